```python
import math
import jax, jax.numpy as jnp
from jax import lax
import numpy as np

D_MODEL = 1024
BATCH = 8
SEQ = 4096
DEPTH = 1

ATTN_HEADS = 8
ATTN_HEAD_DIM = 64
ATTN_WIDTH = ATTN_HEADS * ATTN_HEAD_DIM
MOBA_BLOCK = 256
MOBA_TOPK = 3
Q_CHUNK = 32
POOL_WIDTH = D_MODEL // 2
POOL_WINDOWS = (2, 4, 8, 16)
POOL_GROUPS = len(POOL_WINDOWS)
POOL_GROUP_DIM = POOL_WIDTH // POOL_GROUPS
IN_WIDTH = 3 * ATTN_WIDTH + POOL_WIDTH + 2 * D_MODEL
PEER_HEADS = 8
PEER_KEYS = 128
PEER_EXPERTS = PEER_KEYS * PEER_KEYS
PEER_QUERY_DIM = 256
PEER_HALF = PEER_QUERY_DIM // 2
PEER_TOPK = 16
PEER_CHUNK = 128
RMS_EPS = 1e-6
NEG_INF = -1e30

kernel_name = "hybrid_moba_pool_peer_block"


def rms_norm(x, gain):
    xf = x.astype(jnp.float32)
    y = xf * lax.rsqrt(jnp.mean(xf * xf, axis=-1, keepdims=True) + RMS_EPS)
    return (y * gain.astype(jnp.float32)).astype(x.dtype)


def modulate(h, shift, scale):
    return h * (1.0 + scale[:, None, :]) + shift[:, None, :]


def alibi_slopes(n_heads):
    return jnp.asarray(2.0 ** (-8.0 * (np.arange(n_heads) + 1) / n_heads), dtype=jnp.float32)


def moba_attention(q, k, v):
    B, H, S, hd = q.shape
    nb = -(-S // MOBA_BLOCK)
    s_pad = nb * MOBA_BLOCK
    pad = ((0, 0), (0, 0), (0, s_pad - S), (0, 0))
    kp = jnp.pad(k, pad)
    vp = jnp.pad(v, pad)
    kb = kp.reshape(B, H, nb, MOBA_BLOCK, hd)
    vb = vp.reshape(B, H, nb, MOBA_BLOCK, hd)
    k_mean = jnp.mean(kb.astype(jnp.float32), axis=3)
    n_sel = min(MOBA_TOPK, nb - 1)
    slopes = alibi_slopes(H)[None, :, None, None]
    scale = hd ** -0.5
    b_ix = jnp.arange(B)[:, None, None, None]
    h_ix = jnp.arange(H)[None, :, None, None]
    in_block = jnp.arange(MOBA_BLOCK)

    def chunk(start):
        qc = lax.dynamic_slice_in_dim(q, start, Q_CHUNK, axis=2).astype(jnp.float32) * scale
        q_pos = start + jnp.arange(Q_CHUNK)
        blk = start // MOBA_BLOCK
        own_start = blk * MOBA_BLOCK
        k_own = lax.dynamic_slice_in_dim(kp, own_start, MOBA_BLOCK, axis=2).astype(jnp.float32)
        v_own = lax.dynamic_slice_in_dim(vp, own_start, MOBA_BLOCK, axis=2).astype(jnp.float32)
        dist_own = q_pos[:, None] - (own_start + in_block)[None, :]
        lg_own = jnp.einsum('bhqd,bhkd->bhqk', qc, k_own) - slopes * dist_own
        lg_own = jnp.where(dist_own >= 0, lg_own, NEG_INF)
        if n_sel == 0:
            p_own = jax.nn.softmax(lg_own, axis=-1)
            out = jnp.einsum('bhqk,bhkd->bhqd', p_own, v_own)
            return out.astype(q.dtype)
        gate = jnp.einsum('bhqd,bhnd->bhqn', qc, k_mean)
        gate = jnp.where(jnp.arange(nb) < blk, gate, NEG_INF)
        _, sel = lax.top_k(gate, n_sel)
        sel_valid = jnp.arange(n_sel) < blk
        k_sel = kb[b_ix, h_ix, sel].astype(jnp.float32)
        v_sel = vb[b_ix, h_ix, sel].astype(jnp.float32)
        dist_sel = q_pos[None, None, :, None, None] - (sel[..., None] * MOBA_BLOCK + in_block)
        lg_sel = jnp.einsum('bhqd,bhqnkd->bhqnk', qc, k_sel) - slopes[..., None] * dist_sel
        lg_sel = jnp.where(sel_valid[:, None], lg_sel, NEG_INF)
        lg = jnp.concatenate([lg_sel.reshape(B, H, Q_CHUNK, n_sel * MOBA_BLOCK), lg_own], axis=-1)
        p = jax.nn.softmax(lg, axis=-1)
        p_sel = p[..., :n_sel * MOBA_BLOCK].reshape(B, H, Q_CHUNK, n_sel, MOBA_BLOCK)
        p_own = p[..., n_sel * MOBA_BLOCK:]
        out = (jnp.einsum('bhqnk,bhqnkd->bhqd', p_sel, v_sel)
               + jnp.einsum('bhqk,bhkd->bhqd', p_own, v_own))
        return out.astype(q.dtype)

    starts = jnp.arange(S // Q_CHUNK) * Q_CHUNK
    outs = lax.map(chunk, starts)
    return outs.transpose(1, 2, 0, 3, 4).reshape(B, H, S, hd)


def multiscale_pool(p, w_mix, b_mix, layer_scale):
    B, S, C = p.shape
    pf = p.astype(jnp.float32).reshape(B, S, POOL_GROUPS, POOL_GROUP_DIM)
    cs0 = jnp.pad(jnp.cumsum(pf, axis=1), ((0, 0), (1, 0), (0, 0), (0, 0)))
    t = jnp.arange(S)
    pooled = []
    for g, w in enumerate(POOL_WINDOWS):
        csg = cs0[:, :, g]
        win = csg[:, 1:] - csg[:, jnp.maximum(t + 1 - w, 0)]
        cnt = jnp.minimum(t + 1, w).astype(jnp.float32)[None, :, None]
        pooled.append(win / cnt - pf[:, :, g])
    pooled = jnp.stack(pooled, axis=2)
    mixed = jnp.einsum('bsgc,gcd->bsgd', pooled, w_mix.astype(jnp.float32)) + b_mix.astype(jnp.float32)
    return (mixed.reshape(B, S, C) * layer_scale.astype(jnp.float32)).astype(p.dtype)


def peer_ffn(h, w_q, sub_keys, expert_u, expert_v):
    B, S, D = h.shape
    T = B * S
    ht = h.reshape(T, D)
    q = (ht @ w_q).reshape(T, PEER_HEADS, 2, PEER_HALF).astype(jnp.float32)
    s = jnp.einsum('thpd,pnd->thpn', q, sub_keys.astype(jnp.float32))
    v1, i1 = lax.top_k(s[:, :, 0], PEER_TOPK)
    v2, i2 = lax.top_k(s[:, :, 1], PEER_TOPK)
    cand = (v1[..., :, None] + v2[..., None, :]).reshape(T, PEER_HEADS, PEER_TOPK * PEER_TOPK)
    top_s, flat = lax.top_k(cand, PEER_TOPK)
    e1 = jnp.take_along_axis(i1, flat // PEER_TOPK, axis=-1)
    e2 = jnp.take_along_axis(i2, flat % PEER_TOPK, axis=-1)
    expert_idx = e1 * PEER_KEYS + e2
    gates = jax.nn.softmax(top_s, axis=-1)
    n_chunks = T // PEER_CHUNK

    def chunk(args):
        hc, idx, g = args
        a = jnp.einsum('cd,chkd->chk', hc, expert_u[idx]).astype(jnp.float32)
        act = (jax.nn.gelu(a) * g).astype(h.dtype)
        return jnp.einsum('chk,chkd->cd', act, expert_v[idx])

    out = lax.map(chunk, (ht.reshape(n_chunks, PEER_CHUNK, D),
                          expert_idx.reshape(n_chunks, PEER_CHUNK, PEER_HEADS, PEER_TOPK),
                          gates.reshape(n_chunks, PEER_CHUNK, PEER_HEADS, PEER_TOPK)))
    return out.reshape(B, S, D)


def setup_inputs(seed: int = 0) -> dict:
    key = jax.random.key(seed)
    ks = jax.random.split(key, 20)
    D = D_MODEL
    nrm = lambda k, shape, s: jax.random.normal(k, shape, dtype=jnp.float32) * s
    return {
        "x": nrm(ks[0], (BATCH, SEQ, D), 1.0),
        "c": nrm(ks[1], (BATCH, D), 1.0),
        "w_ada": nrm(ks[2], (DEPTH, D, 6 * D), 0.5 * D ** -0.5),
        "b_ada": nrm(ks[3], (DEPTH, 6 * D), 0.02),
        "g_norm1": 1.0 + nrm(ks[4], (DEPTH, D), 0.02),
        "w_in": nrm(ks[5], (DEPTH, D, IN_WIDTH), D ** -0.5),
        "w_attn_up": nrm(ks[6], (DEPTH, ATTN_WIDTH, D), ATTN_WIDTH ** -0.5),
        "w_pool_mix": nrm(ks[7], (DEPTH, POOL_GROUPS, POOL_GROUP_DIM, POOL_GROUP_DIM), POOL_GROUP_DIM ** -0.5),
        "b_pool_mix": nrm(ks[8], (DEPTH, POOL_GROUPS, POOL_GROUP_DIM), 0.02),
        "pool_scale": 1.0 + nrm(ks[9], (DEPTH, POOL_WIDTH), 0.05),
        "w_pool_up": nrm(ks[10], (DEPTH, POOL_WIDTH, D), POOL_WIDTH ** -0.5),
        "w_out": nrm(ks[11], (DEPTH, D, D), D ** -0.5),
        "g_norm2": 1.0 + nrm(ks[12], (DEPTH, D), 0.02),
        "w_peer_q": nrm(ks[13], (DEPTH, D, PEER_HEADS * PEER_QUERY_DIM), D ** -0.5),
        "peer_sub_keys": nrm(ks[14], (DEPTH, 2, PEER_KEYS, PEER_HALF), PEER_HALF ** -0.5),
        "peer_u": nrm(ks[15], (DEPTH, PEER_EXPERTS, D), D ** -0.5),
        "peer_v": nrm(ks[16], (DEPTH, PEER_EXPERTS, D), 0.5 * PEER_HEADS ** -0.5),
        "g_final": 1.0 + nrm(ks[17], (D,), 0.02),
    }


def reference(x, c, w_ada, b_ada, g_norm1, w_in, w_attn_up, w_pool_mix, b_pool_mix, pool_scale,
              w_pool_up, w_out, g_norm2, w_peer_q, peer_sub_keys, peer_u, peer_v, g_final):
    B, S, D = x.shape
    A = ATTN_WIDTH
    P = POOL_WIDTH
    split_at = [A, 2 * A, 3 * A, 3 * A + P, 3 * A + P + D]

    def heads(t):
        return t.reshape(B, S, ATTN_HEADS, ATTN_HEAD_DIM).transpose(0, 2, 1, 3)

    c_act = jax.nn.silu(c)
    for l in range(DEPTH):
        mod = c_act @ w_ada[l] + b_ada[l]
        sh1, sc1, gt1, sh2, sc2, gt2 = jnp.split(mod, 6, axis=-1)
        h = modulate(rms_norm(x, g_norm1[l]), sh1, sc1)
        proj = h @ w_in[l]
        q, k, v, p, ga, gb = jnp.split(proj, split_at, axis=-1)
        attn = moba_attention(heads(q), heads(k), heads(v)).transpose(0, 2, 1, 3).reshape(B, S, A)
        pool = multiscale_pool(p, w_pool_mix[l], b_pool_mix[l], pool_scale[l])
        merged = (jax.nn.sigmoid(ga) * (attn @ w_attn_up[l])
                  + jax.nn.sigmoid(gb) * (pool @ w_pool_up[l]))
        x = x + gt1[:, None, :] * (merged @ w_out[l])
        h2 = modulate(rms_norm(x, g_norm2[l]), sh2, sc2)
        x = x + gt2[:, None, :] * peer_ffn(h2, w_peer_q[l], peer_sub_keys[l], peer_u[l], peer_v[l])
    return rms_norm(x, g_final)
```

```python
import functools

import jax
import jax.numpy as jnp
import numpy as np
from jax import lax
from jax.experimental import pallas as pl
from jax.experimental.pallas import tpu as pltpu

F32 = jnp.float32
BF16 = jnp.bfloat16

ATTN_HEADS = 8
HEAD_DIM = 64
ATTN_WIDTH = ATTN_HEADS * HEAD_DIM
MOBA_BLOCK = 256
MOBA_TOPK = 3
POOL_WINDOWS = (2, 4, 8, 16)
POOL_GROUP_DIM = 128
POOL_WIDTH = len(POOL_WINDOWS) * POOL_GROUP_DIM
POOL_HALO = 16
PEER_HEADS = 8
PEER_KEYS = 128
PEER_HALF = 128
PEER_TOPK = 16
RMS_EPS = 1e-6
NEG_INF = -1e30
MASKED_SHIFT = 3e38

LANES = 128
VMEM_LIMIT = 56 * 1024 * 1024

TOKEN_TILE = 512
EXPERT_TILE = 1024

NT_DIMS = (((1,), (1,)), ((), ()))


def _sigmoid(x):
    return 1.0 / (1.0 + jnp.exp(-x))


def _gelu_tanh(x):
    c = np.sqrt(2.0 / np.pi).astype(np.float32)
    return 0.5 * x * (1.0 + jnp.tanh(c * (x + 0.044715 * (x * x * x))))


def _rms_norm(x, gain):
    ms = jnp.mean(x * x, axis=-1, keepdims=True)
    return x * lax.rsqrt(ms + RMS_EPS) * gain


def _params(n_axes):
    return pltpu.CompilerParams(dimension_semantics=("arbitrary",) * n_axes,
                                vmem_limit_bytes=VMEM_LIMIT)


def _adaln_kernel(c_ref, w_ref, b_ref, o_ref):
    c = c_ref[...]
    ca = c * _sigmoid(c)
    o_ref[...] = jnp.dot(ca, w_ref[...], preferred_element_type=F32,
                         precision=lax.Precision.HIGHEST) + b_ref[...]


def _adaln(c, w, b):
    bsz, d = c.shape
    n = w.shape[1]
    tn = 1536
    return pl.pallas_call(
        _adaln_kernel,
        out_shape=jax.ShapeDtypeStruct((bsz, n), F32),
        grid=(n // tn,),
        in_specs=[pl.BlockSpec((bsz, d), lambda j: (0, 0)),
                  pl.BlockSpec((d, tn), lambda j: (0, j)),
                  pl.BlockSpec((1, tn), lambda j: (0, j))],
        out_specs=pl.BlockSpec((bsz, tn), lambda j: (0, j)),
        compiler_params=_params(1),
        name="adaln",
    )(c, w, b.reshape(1, n))


def _inproj_kernel(x_ref, g_ref, sh_ref, sc_ref, w_ref,
                   q_ref, k_ref, v_ref, p_ref, ga_ref, gb_ref, km_ref):
    d = x_ref.shape[-1]
    a = ATTN_WIDTH
    h = _rms_norm(x_ref[0], g_ref[...]) * (1.0 + sc_ref[0]) + sh_ref[0]
    hb = h.astype(BF16)

    def proj(lo, hi):
        return jnp.dot(hb, w_ref[:, lo:hi], preferred_element_type=F32)

    q_ref[0] = (proj(0, a) * (HEAD_DIM ** -0.5)).astype(BF16)
    k = proj(a, 2 * a)
    k_ref[0] = k.astype(BF16)
    for r in range(k.shape[0] // MOBA_BLOCK):
        km_ref[0, r] = jnp.mean(k[r * MOBA_BLOCK:(r + 1) * MOBA_BLOCK], axis=0, keepdims=True)
    v_ref[0] = proj(2 * a, 3 * a).astype(BF16)
    p_ref[0] = proj(3 * a, 3 * a + POOL_WIDTH)
    ga_ref[0] = proj(3 * a + POOL_WIDTH, 3 * a + POOL_WIDTH + d)
    gb_ref[0] = proj(3 * a + POOL_WIDTH + d, 3 * a + POOL_WIDTH + 2 * d)


def _inproj(x, g1, sh1, sc1, w_in):
    bsz, s, d = x.shape
    tm = TOKEN_TILE
    nbt = tm // MOBA_BLOCK
    row = lambda b, t: (b, t, 0)
    per_b = lambda b, t: (b, 0, 0)
    out_shape = (
        jax.ShapeDtypeStruct((bsz, s, ATTN_WIDTH), BF16),
        jax.ShapeDtypeStruct((bsz, s, ATTN_WIDTH), BF16),
        jax.ShapeDtypeStruct((bsz, s, ATTN_WIDTH), BF16),
        jax.ShapeDtypeStruct((bsz, s, POOL_WIDTH), F32),
        jax.ShapeDtypeStruct((bsz, s, d), F32),
        jax.ShapeDtypeStruct((bsz, s, d), F32),
        jax.ShapeDtypeStruct((bsz, s // MOBA_BLOCK, 1, ATTN_WIDTH), F32),
    )
    return pl.pallas_call(
        _inproj_kernel,
        out_shape=out_shape,
        grid=(bsz, s // tm),
        in_specs=[pl.BlockSpec((1, tm, d), row),
                  pl.BlockSpec((1, d), lambda b, t: (0, 0)),
                  pl.BlockSpec((1, 1, d), per_b),
                  pl.BlockSpec((1, 1, d), per_b),
                  pl.BlockSpec(w_in.shape, lambda b, t: (0, 0))],
        out_specs=(pl.BlockSpec((1, tm, ATTN_WIDTH), row),
                   pl.BlockSpec((1, tm, ATTN_WIDTH), row),
                   pl.BlockSpec((1, tm, ATTN_WIDTH), row),
                   pl.BlockSpec((1, tm, POOL_WIDTH), row),
                   pl.BlockSpec((1, tm, d), row),
                   pl.BlockSpec((1, tm, d), row),
                   pl.BlockSpec((1, nbt, 1, ATTN_WIDTH), lambda b, t: (b, t, 0, 0))),
        compiler_params=_params(2),
        name="inproj",
    )(x, g1.reshape(1, d), sh1, sc1, w_in)


def _moba_kernel(slopes_ref, q_ref, k_ref, v_ref, km_ref, o_ref):
    blk = MOBA_BLOCK
    hp = pl.program_id(1)
    qb = pl.program_id(2)
    rel = (lax.broadcasted_iota(jnp.int32, (blk, blk), 0)
           - lax.broadcasted_iota(jnp.int32, (blk, blk), 1)).astype(F32)
    lane = lax.broadcasted_iota(jnp.int32, (blk, LANES), 1)
    lane_f = lane.astype(F32)
    n_heads_here = q_ref.shape[-1] // HEAD_DIM
    outs = []
    for hh in range(n_heads_here):
        cols = slice(hh * HEAD_DIM, (hh + 1) * HEAD_DIM)
        slope = slopes_ref[hp * n_heads_here + hh]
        qh = q_ref[0, :, cols]
        srel = slope * rel

        kmh = km_ref[0, :, cols].astype(BF16)
        gate = lax.dot_general(qh, kmh, NT_DIMS, preferred_element_type=F32)
        gate = jnp.where(lane < qb, gate, -jnp.inf)
        sel = jnp.zeros((blk, LANES), F32)
        for _ in range(MOBA_TOPK):
            gmax = jnp.max(gate, axis=1, keepdims=True)
            first = jnp.min(jnp.where(gate == gmax, lane_f, float(LANES)), axis=1, keepdims=True)
            first = jnp.where(gmax == -jnp.inf, float(LANES), first)
            pick = lane_f == first
            sel = jnp.where(pick, 1.0, sel)
            gate = jnp.where(pick, -jnp.inf, gate)

        own = pl.multiple_of(qb * blk, blk)
        k_own = k_ref[0, pl.ds(own, blk), cols]
        v_own = v_ref[0, pl.ds(own, blk), cols]
        t = lax.dot_general(qh, k_own, NT_DIMS, preferred_element_type=F32) - srel
        t = jnp.where(rel >= 0.0, t, NEG_INF)
        m0 = jnp.max(t, axis=1, keepdims=True)
        p = jnp.exp(t - m0)
        l0 = jnp.sum(p, axis=1, keepdims=True)
        acc0 = jnp.dot(p.astype(BF16), v_own, preferred_element_type=F32)

        def past_block(j, carry, qh=qh, srel=srel, slope=slope, sel=sel, cols=cols):
            m, l, acc = carry
            start = pl.multiple_of(j * blk, blk)
            kj = k_ref[0, pl.ds(start, blk), cols]
            vj = v_ref[0, pl.ds(start, blk), cols]
            t = lax.dot_general(qh, kj, NT_DIMS, preferred_element_type=F32) - srel
            block_bias = slope * ((qb - j) * blk).astype(F32)
            sel_j = jnp.sum(jnp.where(lane == j, sel, 0.0), axis=1, keepdims=True) > 0.5
            mb = jnp.max(t, axis=1, keepdims=True) - block_bias
            m_new = jnp.maximum(m, jnp.where(sel_j, mb, -jnp.inf))
            shift = jnp.where(sel_j, m_new + block_bias, MASKED_SHIFT)
            p = jnp.exp(t - shift)
            alpha = jnp.exp(m - m_new)
            l = alpha * l + jnp.sum(p, axis=1, keepdims=True)
            acc = alpha * acc + jnp.dot(p.astype(BF16), vj, preferred_element_type=F32)
            return m_new, l, acc

        _, l, acc = lax.fori_loop(0, qb, past_block, (m0, l0, acc0))
        outs.append(acc / l)
    o_ref[0] = jnp.concatenate(outs, axis=-1).astype(o_ref.dtype)


def _moba(q, k, v, kmean_padded):
    bsz, s, a = q.shape
    blk = MOBA_BLOCK
    nb = s // blk
    hw = 2 * HEAD_DIM
    slopes = jnp.asarray(2.0 ** (-8.0 * (np.arange(ATTN_HEADS) + 1) / ATTN_HEADS), dtype=F32)
    return pl.pallas_call(
        _moba_kernel,
        out_shape=jax.ShapeDtypeStruct((bsz, s, a), BF16),
        grid=(bsz, a // hw, nb),
        in_specs=[pl.BlockSpec(memory_space=pltpu.SMEM),
                  pl.BlockSpec((1, blk, hw), lambda b, h, i: (b, i, h)),
                  pl.BlockSpec((1, s, hw), lambda b, h, i: (b, 0, h)),
                  pl.BlockSpec((1, s, hw), lambda b, h, i: (b, 0, h)),
                  pl.BlockSpec((1, LANES, hw), lambda b, h, i: (b, 0, h))],
        out_specs=pl.BlockSpec((1, blk, hw), lambda b, h, i: (b, i, h)),
        compiler_params=_params(3),
        name="moba",
    )(slopes, q, k, v, kmean_padded)


def _mix_kernel(x_ref, attn_ref, p_ref, pprev_ref, ga_ref, gb_ref,
                wau_ref, wmix_ref, bmix_ref, psc_ref, wpu_ref, wout_ref,
                gt1_ref, g2_ref, sh2_ref, sc2_ref,
                x1_ref, h2t_ref, hist_ref):
    tm = x_ref.shape[1]
    t = pl.program_id(1)
    hist_ref[0:POOL_HALO, :] = jnp.where(t == 0, 0.0, pprev_ref[0])
    hist_ref[POOL_HALO:, :] = p_ref[0]
    pos = t * tm + lax.broadcasted_iota(jnp.int32, (tm, 1), 0)
    mixed = []
    for g, w in enumerate(POOL_WINDOWS):
        cols = slice(g * POOL_GROUP_DIM, (g + 1) * POOL_GROUP_DIM)
        win = hist_ref[POOL_HALO - (w - 1):POOL_HALO - (w - 1) + tm, cols]
        for dlt in range(w - 2, -1, -1):
            win = win + hist_ref[POOL_HALO - dlt:POOL_HALO - dlt + tm, cols]
        cnt = jnp.minimum(pos + 1, w).astype(F32)
        pooled = win / cnt - hist_ref[POOL_HALO:, cols]
        mg = jnp.dot(pooled.astype(BF16), wmix_ref[g], preferred_element_type=F32) + bmix_ref[g]
        mixed.append(mg * psc_ref[:, cols])
    pool = jnp.concatenate(mixed, axis=-1).astype(BF16)
    up_a = jnp.dot(attn_ref[0], wau_ref[...], preferred_element_type=F32)
    up_b = jnp.dot(pool, wpu_ref[...], preferred_element_type=F32)
    merged = _sigmoid(ga_ref[0]) * up_a + _sigmoid(gb_ref[0]) * up_b
    x1 = x_ref[0] + gt1_ref[0] * jnp.dot(merged.astype(BF16), wout_ref[...], preferred_element_type=F32)
    x1_ref[0] = x1
    h2 = _rms_norm(x1, g2_ref[...]) * (1.0 + sc2_ref[0]) + sh2_ref[0]
    h2t_ref[...] = h2.T.astype(BF16)


def _mix(x, attn, p, ga, gb, wau, wmix, bmix, psc, wpu, wout, gt1, g2, sh2, sc2):
    bsz, s, d = x.shape
    tm = TOKEN_TILE
    nt = s // tm
    row = lambda b, t: (b, t, 0)
    per_b = lambda b, t: (b, 0, 0)
    whole2 = lambda b, t: (0, 0)
    whole3 = lambda b, t: (0, 0, 0)
    halo_blocks = tm // POOL_HALO
    return pl.pallas_call(
        _mix_kernel,
        out_shape=(jax.ShapeDtypeStruct((bsz, s, d), F32),
                   jax.ShapeDtypeStruct((d, bsz * s), BF16)),
        grid=(bsz, nt),
        in_specs=[pl.BlockSpec((1, tm, d), row),
                  pl.BlockSpec((1, tm, ATTN_WIDTH), row),
                  pl.BlockSpec((1, tm, POOL_WIDTH), row),
                  pl.BlockSpec((1, POOL_HALO, POOL_WIDTH),
                               lambda b, t: (b, jnp.maximum(t * halo_blocks - 1, 0), 0)),
                  pl.BlockSpec((1, tm, d), row),
                  pl.BlockSpec((1, tm, d), row),
                  pl.BlockSpec(wau.shape, whole2),
                  pl.BlockSpec(wmix.shape, whole3),
                  pl.BlockSpec(bmix.shape, whole3),
                  pl.BlockSpec(psc.shape, whole2),
                  pl.BlockSpec(wpu.shape, whole2),
                  pl.BlockSpec(wout.shape, whole2),
                  pl.BlockSpec((1, 1, d), per_b),
                  pl.BlockSpec((1, d), whole2),
                  pl.BlockSpec((1, 1, d), per_b),
                  pl.BlockSpec((1, 1, d), per_b)],
        out_specs=(pl.BlockSpec((1, tm, d), row),
                   pl.BlockSpec((d, tm), lambda b, t: (0, b * nt + t))),
        scratch_shapes=[pltpu.VMEM((tm + POOL_HALO, POOL_WIDTH), F32)],
        compiler_params=_params(2),
        name="mix",
    )(x, attn, p, p, ga, gb, wau, wmix, bmix, psc, wpu, wout, gt1, g2, sh2, sc2)


def _staircase(n):
    return [(a, b) for a in range(n) for b in range(n) if (a + 1) * (b + 1) <= n]


def _peer_select_kernel(h2t_ref, wqt_ref, keys_ref,
                        s2_ref, bb_ref, th_ref, aa_ref,
                        s_scr, top_scr):
    n_top = PEER_TOPK + 1
    h2t = h2t_ref[...]
    qdim = 2 * PEER_HALF
    for h in range(PEER_HEADS):
        qt = jnp.dot(wqt_ref[h * qdim:(h + 1) * qdim, :], h2t, preferred_element_type=F32).astype(BF16)
        for part in range(2):
            st = jnp.dot(keys_ref[part], qt[part * PEER_HALF:(part + 1) * PEER_HALF],
                         preferred_element_type=F32)
            s_scr[part, h] = st

            def extract(r, w, part=part, h=h):
                top = jnp.max(w, axis=0, keepdims=True)
                top_scr[part, pl.ds(r, 1), h:h + 1, :] = top[None]
                return jnp.where(w == top, -jnp.inf, w)

            lax.fori_loop(0, n_top, extract, st)

    v1 = [top_scr[0, r] for r in range(n_top)]
    v2 = [top_scr[1, r] for r in range(n_top)]
    cands = [v1[a] + v2[b] for a, b in _staircase(n_top)]
    best = None
    z = None
    ordered = []
    for r in range(n_top):
        cur = functools.reduce(jnp.maximum, cands)
        ordered.append(cur)
        if r == 0:
            best = cur
            z = jnp.ones_like(cur)
        elif r < PEER_TOPK:
            z = z + jnp.exp(cur - best)
        if r + 1 < n_top:
            cands = [jnp.where(c == cur, -jnp.inf, c) for c in cands]
    tau = 0.5 * (ordered[PEER_TOPK - 1] + ordered[PEER_TOPK])
    inv_z = 1.0 / z
    for h in range(PEER_HEADS):
        s1 = s_scr[0, h]
        s2 = s_scr[1, h]
        s2_ref[h] = s2
        bb_ref[h] = jnp.exp(s2 - v2[0][h:h + 1]) * inv_z[h:h + 1]
        th_ref[h] = tau[h:h + 1] - s1
        aa_ref[h] = jnp.exp(s1 - v1[0][h:h + 1])


def _peer_select(h2t, wqt, keys):
    d, tokens = h2t.shape
    tm = TOKEN_TILE
    shape = jax.ShapeDtypeStruct((PEER_HEADS, PEER_KEYS, tokens), F32)
    blk = pl.BlockSpec((PEER_HEADS, PEER_KEYS, tm), lambda t: (0, 0, t))
    return pl.pallas_call(
        _peer_select_kernel,
        out_shape=(shape, shape, shape, shape),
        grid=(tokens // tm,),
        in_specs=[pl.BlockSpec((d, tm), lambda t: (0, t)),
                  pl.BlockSpec(wqt.shape, lambda t: (0, 0)),
                  pl.BlockSpec(keys.shape, lambda t: (0, 0, 0))],
        out_specs=(blk, blk, blk, blk),
        scratch_shapes=[pltpu.VMEM((2, PEER_HEADS, PEER_KEYS, tm), F32),
                        pltpu.VMEM((2, PEER_TOPK + 1, PEER_HEADS, tm), F32)],
        compiler_params=_params(1),
        name="peer_select",
    )(h2t, wqt, keys)


def _peer_expert_kernel(h2t_ref, u_ref, vt_ref, s2_ref, bb_ref, th_ref, aa_ref,
                        x1_ref, gt2_ref, gf_ref, o_ref, acc_ref, act_ref):
    e = pl.program_id(1)
    tm = h2t_ref.shape[1]
    rows_per_step = u_ref.shape[0] // PEER_KEYS

    @pl.when(e == 0)
    def _():
        acc_ref[...] = jnp.zeros_like(acc_ref)

    a_t = jnp.dot(u_ref[...], h2t_ref[...], preferred_element_type=F32)
    for ii in range(rows_per_step):
        for c in range(tm // LANES):
            lanes = slice(c * LANES, (c + 1) * LANES)
            gate = jnp.zeros((PEER_KEYS, LANES), F32)
            for h in range(PEER_HEADS):
                hit = s2_ref[h, :, lanes] >= th_ref[h, ii:ii + 1, lanes]
                gate = gate + jnp.where(hit, bb_ref[h, :, lanes], 0.0) * aa_ref[h, ii:ii + 1, lanes]
            a_blk = a_t[ii * PEER_KEYS:(ii + 1) * PEER_KEYS, lanes]
            act_ref[ii * PEER_KEYS:(ii + 1) * PEER_KEYS, lanes] = (_gelu_tanh(a_blk) * gate).astype(BF16)
    acc_ref[...] += jnp.dot(vt_ref[...], act_ref[...], preferred_element_type=F32)

    @pl.when(e == pl.num_programs(1) - 1)
    def _():
        y = x1_ref[...] + gt2_ref[0] * acc_ref[...].T
        o_ref[...] = _rms_norm(y, gf_ref[...])


def _peer_expert(h2t, u, vt, s2, bb, th, aa, x1, gt2, gf, tokens_per_seq):
    d, tokens = h2t.shape
    n_exp = u.shape[0]
    tm = TOKEN_TILE
    eb = EXPERT_TILE
    rows = eb // PEER_KEYS
    tiles_per_seq = tokens_per_seq // tm
    full = pl.BlockSpec((PEER_HEADS, PEER_KEYS, tm), lambda t, e: (0, 0, t))
    part = pl.BlockSpec((PEER_HEADS, rows, tm), lambda t, e: (0, e, t))
    return pl.pallas_call(
        _peer_expert_kernel,
        out_shape=jax.ShapeDtypeStruct((tokens, d), F32),
        grid=(tokens // tm, n_exp // eb),
        in_specs=[pl.BlockSpec((d, tm), lambda t, e: (0, t)),
                  pl.BlockSpec((eb, d), lambda t, e: (e, 0)),
                  pl.BlockSpec((d, eb), lambda t, e: (0, e)),
                  full, full, part, part,
                  pl.BlockSpec((tm, d), lambda t, e: (t, 0)),
                  pl.BlockSpec((1, 1, d), lambda t, e: (t // tiles_per_seq, 0, 0)),
                  pl.BlockSpec((1, d), lambda t, e: (0, 0))],
        out_specs=pl.BlockSpec((tm, d), lambda t, e: (t, 0)),
        scratch_shapes=[pltpu.VMEM((d, tm), F32),
                        pltpu.VMEM((eb, tm), BF16)],
        compiler_params=_params(2),
        name="peer_expert",
    )(h2t, u, vt, s2, bb, th, aa, x1, gt2, gf)


def kernel(x, c, w_ada, b_ada, g_norm1, w_in, w_attn_up, w_pool_mix, b_pool_mix, pool_scale,
           w_pool_up, w_out, g_norm2, w_peer_q, peer_sub_keys, peer_u, peer_v, g_final):
    bsz, s, d = x.shape
    depth = w_ada.shape[0]
    assert depth == 1, "the final norm is fused into the last PEER step of a single layer"
    assert s % TOKEN_TILE == 0 and TOKEN_TILE % MOBA_BLOCK == 0 and s // MOBA_BLOCK <= LANES
    assert w_in.shape[-1] == 3 * ATTN_WIDTH + POOL_WIDTH + 2 * d
    for l in range(depth):
        mod = _adaln(c, w_ada[l], b_ada[l])
        sh1, sc1, gt1, sh2, sc2, gt2 = [m.reshape(bsz, 1, d) for m in jnp.split(mod, 6, axis=-1)]
        q, k, v, p, ga, gb, kmean = _inproj(x, g_norm1[l], sh1, sc1, w_in[l].astype(BF16))
        nb = s // MOBA_BLOCK
        kmean = jnp.pad(kmean.reshape(bsz, nb, ATTN_WIDTH), ((0, 0), (0, LANES - nb), (0, 0)))
        attn = _moba(q, k, v, kmean)
        x1, h2t = _mix(x, attn, p, ga, gb,
                       w_attn_up[l].astype(BF16), w_pool_mix[l].astype(BF16),
                       b_pool_mix[l].reshape(len(POOL_WINDOWS), 1, POOL_GROUP_DIM),
                       pool_scale[l].reshape(1, POOL_WIDTH), w_pool_up[l].astype(BF16),
                       w_out[l].astype(BF16), gt1, g_norm2[l].reshape(1, d), sh2, sc2)
        s2, bb, th, aa = _peer_select(h2t, w_peer_q[l].T.astype(BF16), peer_sub_keys[l].astype(BF16))
        y = _peer_expert(h2t, peer_u[l].astype(BF16), peer_v[l].T.astype(BF16),
                         s2, bb, th, aa, x1.reshape(bsz * s, d), gt2, g_final.reshape(1, d), s)
        x = y.reshape(bsz, s, d)
    return x
```

```python
import functools

import jax
import jax.numpy as jnp
import numpy as np
from jax import lax
from jax.experimental import pallas as pl
from jax.experimental.pallas import tpu as pltpu

F32 = jnp.float32
BF16 = jnp.bfloat16

ATTN_HEADS = 8
HEAD_DIM = 64
ATTN_WIDTH = ATTN_HEADS * HEAD_DIM
MOBA_BLOCK = 256
MOBA_TOPK = 3
POOL_WINDOWS = (2, 4, 8, 16)
POOL_GROUP_DIM = 128
POOL_WIDTH = len(POOL_WINDOWS) * POOL_GROUP_DIM
POOL_HALO = 16
PEER_HEADS = 8
PEER_KEYS = 128
PEER_HALF = 128
PEER_TOPK = 16
RMS_EPS = 1e-6
NEG_INF = -1e30
MASKED_SHIFT = 3e38

LANES = 128
SUBLANES = 8
MOBA_HEADS_PER_STEP = 4
VMEM_LIMIT = 56 * 1024 * 1024

TOKEN_TILE = 512
EXPERT_TILE = 1024
GATE_ROWS = 32

NT_DIMS = (((1,), (1,)), ((), ()))


def _sigmoid(x):
    return 1.0 / (1.0 + jnp.exp(-x))


def _gelu_tanh(x):
    c = np.sqrt(2.0 / np.pi).astype(np.float32)
    return 0.5 * x * (1.0 + jnp.tanh(c * (x + 0.044715 * (x * x * x))))


def _rms_norm(x, gain):
    ms = jnp.mean(x * x, axis=-1, keepdims=True)
    return x * lax.rsqrt(ms + RMS_EPS) * gain


def _params(n_axes):
    return pltpu.CompilerParams(dimension_semantics=("arbitrary",) * n_axes,
                                vmem_limit_bytes=VMEM_LIMIT)


def _adaln_kernel(c_ref, w_ref, b_ref, o_ref):
    c = c_ref[...]
    ca = c * _sigmoid(c)
    o_ref[...] = jnp.dot(ca, w_ref[...], preferred_element_type=F32,
                         precision=lax.Precision.HIGHEST) + b_ref[...]


def _adaln(c, w, b):
    bsz, d = c.shape
    n = w.shape[1]
    tn = 1536
    return pl.pallas_call(
        _adaln_kernel,
        out_shape=jax.ShapeDtypeStruct((bsz, n), F32),
        grid=(n // tn,),
        in_specs=[pl.BlockSpec((bsz, d), lambda j: (0, 0)),
                  pl.BlockSpec((d, tn), lambda j: (0, j)),
                  pl.BlockSpec((1, tn), lambda j: (0, j))],
        out_specs=pl.BlockSpec((bsz, tn), lambda j: (0, j)),
        compiler_params=_params(1),
        name="adaln",
    )(c, w, b.reshape(1, n))


def _inproj_kernel(x_ref, g_ref, sh_ref, sc_ref, w_ref,
                   q_ref, k_ref, vt_ref, p_ref, ga_ref, gb_ref, km_ref):
    d = x_ref.shape[-1]
    a = ATTN_WIDTH
    h = _rms_norm(x_ref[0], g_ref[...]) * (1.0 + sc_ref[0]) + sh_ref[0]
    hb = h.astype(BF16)

    def proj(lo, hi):
        return jnp.dot(hb, w_ref[:, lo:hi], preferred_element_type=F32)

    q_ref[0] = (proj(0, a) * (HEAD_DIM ** -0.5)).astype(BF16)
    k = proj(a, 2 * a)
    k_ref[0] = k.astype(BF16)
    for r in range(k.shape[0] // MOBA_BLOCK):
        km_ref[0, r] = jnp.mean(k[r * MOBA_BLOCK:(r + 1) * MOBA_BLOCK], axis=0, keepdims=True)
    v = proj(2 * a, 3 * a)
    for r in range(v.shape[0] // MOBA_BLOCK):
        vt_ref[0, r] = v[r * MOBA_BLOCK:(r + 1) * MOBA_BLOCK].T.astype(BF16)
    p_ref[0] = proj(3 * a, 3 * a + POOL_WIDTH)
    ga_ref[0] = proj(3 * a + POOL_WIDTH, 3 * a + POOL_WIDTH + d)
    gb_ref[0] = proj(3 * a + POOL_WIDTH + d, 3 * a + POOL_WIDTH + 2 * d)


def _inproj(x, g1, sh1, sc1, w_in):
    bsz, s, d = x.shape
    tm = TOKEN_TILE
    nbt = tm // MOBA_BLOCK
    row = lambda b, t: (b, t, 0)
    per_b = lambda b, t: (b, 0, 0)
    out_shape = (
        jax.ShapeDtypeStruct((bsz, s, ATTN_WIDTH), BF16),
        jax.ShapeDtypeStruct((bsz, s, ATTN_WIDTH), BF16),
        jax.ShapeDtypeStruct((bsz, s // MOBA_BLOCK, ATTN_WIDTH, MOBA_BLOCK), BF16),
        jax.ShapeDtypeStruct((bsz, s, POOL_WIDTH), F32),
        jax.ShapeDtypeStruct((bsz, s, d), F32),
        jax.ShapeDtypeStruct((bsz, s, d), F32),
        jax.ShapeDtypeStruct((bsz, s // MOBA_BLOCK, 1, ATTN_WIDTH), F32),
    )
    return pl.pallas_call(
        _inproj_kernel,
        out_shape=out_shape,
        grid=(bsz, s // tm),
        in_specs=[pl.BlockSpec((1, tm, d), row),
                  pl.BlockSpec((1, d), lambda b, t: (0, 0)),
                  pl.BlockSpec((1, 1, d), per_b),
                  pl.BlockSpec((1, 1, d), per_b),
                  pl.BlockSpec(w_in.shape, lambda b, t: (0, 0))],
        out_specs=(pl.BlockSpec((1, tm, ATTN_WIDTH), row),
                   pl.BlockSpec((1, tm, ATTN_WIDTH), row),
                   pl.BlockSpec((1, nbt, ATTN_WIDTH, MOBA_BLOCK), lambda b, t: (b, t, 0, 0)),
                   pl.BlockSpec((1, tm, POOL_WIDTH), row),
                   pl.BlockSpec((1, tm, d), row),
                   pl.BlockSpec((1, tm, d), row),
                   pl.BlockSpec((1, nbt, 1, ATTN_WIDTH), lambda b, t: (b, t, 0, 0))),
        compiler_params=_params(2),
        name="inproj",
    )(x, g1.reshape(1, d), sh1, sc1, w_in)


def _moba_kernel(slopes_ref, q_ref, k_ref, vt_ref, km_ref, o_ref, sel_ref, srel_ref):
    blk = MOBA_BLOCK
    hg = pl.program_id(1)
    qb = pl.program_id(2)
    n_here = q_ref.shape[-1] // HEAD_DIM
    nbp = km_ref.shape[1]
    rel_t = (lax.broadcasted_iota(jnp.int32, (blk, blk), 0)
             - lax.broadcasted_iota(jnp.int32, (blk, blk), 1)).astype(F32)
    blk_row = lax.broadcasted_iota(jnp.int32, (nbp, blk), 0)
    blk_row_f = blk_row.astype(F32)
    own = pl.multiple_of(qb * blk, blk)
    head_cols = [slice(hh * HEAD_DIM, (hh + 1) * HEAD_DIM) for hh in range(n_here)]
    slopes = [slopes_ref[hg * n_here + hh] for hh in range(n_here)]
    q_heads = [q_ref[0, :, cols] for cols in head_cols]
    state = []
    for hh in range(n_here):
        cols, slope, qh = head_cols[hh], slopes[hh], q_heads[hh]
        srel = slope * rel_t
        srel_ref[hh] = srel

        kmh = km_ref[0, :, cols].astype(BF16)
        gate = lax.dot_general(kmh, qh, NT_DIMS, preferred_element_type=F32)
        gate = jnp.where(blk_row < qb, gate, -jnp.inf)
        sel = jnp.zeros((nbp, blk), F32)
        for _ in range(MOBA_TOPK):
            gmax = jnp.max(gate, axis=0, keepdims=True)
            first = jnp.min(jnp.where(gate == gmax, blk_row_f, float(nbp)), axis=0, keepdims=True)
            first = jnp.where(gmax == -jnp.inf, float(nbp), first)
            pick = blk_row_f == first
            sel = jnp.where(pick, 1.0, sel)
            gate = jnp.where(pick, -jnp.inf, gate)
        for r in range(nbp):
            sel_ref[hh, r] = sel[r:r + 1, :]

        k_own = k_ref[0, pl.ds(own, blk), cols]
        t = lax.dot_general(k_own, qh, NT_DIMS, preferred_element_type=F32) + srel
        t = jnp.where(rel_t <= 0.0, t, NEG_INF)
        m0 = jnp.max(t, axis=0, keepdims=True)
        p = jnp.exp(t - m0)
        l0 = jnp.sum(p, axis=0, keepdims=True)
        acc0 = jnp.dot(vt_ref[0, qb, cols, :], p.astype(BF16), preferred_element_type=F32)
        state += [m0, l0, acc0]

    def past_block(j, carry):
        start = pl.multiple_of(j * blk, blk)
        scores = [lax.dot_general(k_ref[0, pl.ds(start, blk), cols], q_heads[hh], NT_DIMS,
                                  preferred_element_type=F32) for hh, cols in enumerate(head_cols)]
        probs, scaled = [], []
        for hh in range(n_here):
            m, l, _ = carry[3 * hh:3 * hh + 3]
            t = scores[hh] + srel_ref[hh]
            block_bias = slopes[hh] * ((qb - j) * blk).astype(F32)
            sel_j = sel_ref[hh, j] > 0.5
            mb = jnp.max(t, axis=0, keepdims=True) - block_bias
            m_new = jnp.maximum(m, jnp.where(sel_j, mb, -jnp.inf))
            shift = jnp.where(sel_j, m_new + block_bias, MASKED_SHIFT)
            p = jnp.exp(t - shift)
            alpha = jnp.exp(m - m_new)
            probs.append(p.astype(BF16))
            scaled.append((m_new, alpha * l + jnp.sum(p, axis=0, keepdims=True), alpha))
        nxt = []
        for hh, cols in enumerate(head_cols):
            m_new, l_new, alpha = scaled[hh]
            pv = jnp.dot(vt_ref[0, j, cols, :], probs[hh], preferred_element_type=F32)
            nxt += [m_new, l_new, alpha * carry[3 * hh + 2] + pv]
        return tuple(nxt)

    final = lax.fori_loop(0, qb, past_block, tuple(state))
    out_t = jnp.concatenate([final[3 * hh + 2] / final[3 * hh + 1] for hh in range(n_here)], axis=0)
    o_ref[0] = out_t.T.astype(o_ref.dtype)


def _moba(q, k, vt, kmean):
    bsz, s, a = q.shape
    blk = MOBA_BLOCK
    nb = s // blk
    nbp = kmean.shape[1]
    hw = MOBA_HEADS_PER_STEP * HEAD_DIM
    slopes = jnp.asarray(2.0 ** (-8.0 * (np.arange(ATTN_HEADS) + 1) / ATTN_HEADS), dtype=F32)
    return pl.pallas_call(
        _moba_kernel,
        out_shape=jax.ShapeDtypeStruct((bsz, s, a), BF16),
        grid=(bsz, a // hw, nb),
        in_specs=[pl.BlockSpec(memory_space=pltpu.SMEM),
                  pl.BlockSpec((1, blk, hw), lambda b, h, i: (b, i, h)),
                  pl.BlockSpec((1, s, hw), lambda b, h, i: (b, 0, h)),
                  pl.BlockSpec((1, nb, hw, blk), lambda b, h, i: (b, 0, h, 0)),
                  pl.BlockSpec((1, nbp, hw), lambda b, h, i: (b, 0, h))],
        out_specs=pl.BlockSpec((1, blk, hw), lambda b, h, i: (b, i, h)),
        scratch_shapes=[pltpu.VMEM((MOBA_HEADS_PER_STEP, nbp, 1, blk), F32),
                        pltpu.VMEM((MOBA_HEADS_PER_STEP, blk, blk), F32)],
        compiler_params=_params(3),
        name="moba",
    )(slopes, q, k, vt, kmean)


def _mix_kernel(x_ref, attn_ref, p_ref, pprev_ref, ga_ref, gb_ref,
                wau_ref, wmix_ref, bmix_ref, psc_ref, wpu_ref, wout_ref,
                gt1_ref, g2_ref, sh2_ref, sc2_ref,
                x1_ref, h2t_ref, hist_ref):
    tm = x_ref.shape[1]
    t = pl.program_id(1)
    hist_ref[0:POOL_HALO, :] = jnp.where(t == 0, 0.0, pprev_ref[0])
    hist_ref[POOL_HALO:, :] = p_ref[0]
    pos = t * tm + lax.broadcasted_iota(jnp.int32, (tm, 1), 0)
    mixed = []
    for g, w in enumerate(POOL_WINDOWS):
        cols = slice(g * POOL_GROUP_DIM, (g + 1) * POOL_GROUP_DIM)
        win = hist_ref[POOL_HALO - (w - 1):POOL_HALO - (w - 1) + tm, cols]
        for dlt in range(w - 2, -1, -1):
            win = win + hist_ref[POOL_HALO - dlt:POOL_HALO - dlt + tm, cols]
        cnt = jnp.minimum(pos + 1, w).astype(F32)
        pooled = win / cnt - hist_ref[POOL_HALO:, cols]
        mg = jnp.dot(pooled.astype(BF16), wmix_ref[g], preferred_element_type=F32) + bmix_ref[g]
        mixed.append(mg * psc_ref[:, cols])
    pool = jnp.concatenate(mixed, axis=-1).astype(BF16)
    up_a = jnp.dot(attn_ref[0], wau_ref[...], preferred_element_type=F32)
    up_b = jnp.dot(pool, wpu_ref[...], preferred_element_type=F32)
    merged = _sigmoid(ga_ref[0]) * up_a + _sigmoid(gb_ref[0]) * up_b
    x1 = x_ref[0] + gt1_ref[0] * jnp.dot(merged.astype(BF16), wout_ref[...], preferred_element_type=F32)
    x1_ref[0] = x1
    h2 = _rms_norm(x1, g2_ref[...]) * (1.0 + sc2_ref[0]) + sh2_ref[0]
    h2t_ref[...] = h2.T.astype(BF16)


def _mix(x, attn, p, ga, gb, wau, wmix, bmix, psc, wpu, wout, gt1, g2, sh2, sc2):
    bsz, s, d = x.shape
    tm = TOKEN_TILE
    nt = s // tm
    row = lambda b, t: (b, t, 0)
    per_b = lambda b, t: (b, 0, 0)
    whole2 = lambda b, t: (0, 0)
    whole3 = lambda b, t: (0, 0, 0)
    halo_blocks = tm // POOL_HALO
    return pl.pallas_call(
        _mix_kernel,
        out_shape=(jax.ShapeDtypeStruct((bsz, s, d), F32),
                   jax.ShapeDtypeStruct((d, bsz * s), BF16)),
        grid=(bsz, nt),
        in_specs=[pl.BlockSpec((1, tm, d), row),
                  pl.BlockSpec((1, tm, ATTN_WIDTH), row),
                  pl.BlockSpec((1, tm, POOL_WIDTH), row),
                  pl.BlockSpec((1, POOL_HALO, POOL_WIDTH),
                               lambda b, t: (b, jnp.maximum(t * halo_blocks - 1, 0), 0)),
                  pl.BlockSpec((1, tm, d), row),
                  pl.BlockSpec((1, tm, d), row),
                  pl.BlockSpec(wau.shape, whole2),
                  pl.BlockSpec(wmix.shape, whole3),
                  pl.BlockSpec(bmix.shape, whole3),
                  pl.BlockSpec(psc.shape, whole2),
                  pl.BlockSpec(wpu.shape, whole2),
                  pl.BlockSpec(wout.shape, whole2),
                  pl.BlockSpec((1, 1, d), per_b),
                  pl.BlockSpec((1, d), whole2),
                  pl.BlockSpec((1, 1, d), per_b),
                  pl.BlockSpec((1, 1, d), per_b)],
        out_specs=(pl.BlockSpec((1, tm, d), row),
                   pl.BlockSpec((d, tm), lambda b, t: (0, b * nt + t))),
        scratch_shapes=[pltpu.VMEM((tm + POOL_HALO, POOL_WIDTH), F32)],
        compiler_params=_params(2),
        name="mix",
    )(x, attn, p, p, ga, gb, wau, wmix, bmix, psc, wpu, wout, gt1, g2, sh2, sc2)


def _staircase(n):
    return [(a, b) for a in range(n) for b in range(n) if (a + 1) * (b + 1) <= n]


def _peer_select_kernel(h2t_ref, wqt_ref, keys_ref,
                        s2_ref, bb_ref, th_ref, aa_ref,
                        s_scr, top_scr):
    n_top = PEER_TOPK + 1
    h2t = h2t_ref[...]
    qdim = 2 * PEER_HALF
    for h in range(PEER_HEADS):
        qt = jnp.dot(wqt_ref[h * qdim:(h + 1) * qdim, :], h2t, preferred_element_type=F32).astype(BF16)
        for part in range(2):
            st = jnp.dot(keys_ref[part], qt[part * PEER_HALF:(part + 1) * PEER_HALF],
                         preferred_element_type=F32)
            s_scr[part, h] = st

            def extract(r, w, part=part, h=h):
                top = jnp.max(w, axis=0, keepdims=True)
                top_scr[part, pl.ds(r, 1), h:h + 1, :] = top[None]
                return jnp.where(w == top, -jnp.inf, w)

            lax.fori_loop(0, n_top, extract, st)

    v1 = [top_scr[0, r] for r in range(n_top)]
    v2 = [top_scr[1, r] for r in range(n_top)]
    cands = [v1[a] + v2[b] for a, b in _staircase(n_top)]
    best = None
    z = None
    ordered = []
    for r in range(n_top):
        cur = functools.reduce(jnp.maximum, cands)
        ordered.append(cur)
        if r == 0:
            best = cur
            z = jnp.ones_like(cur)
        elif r < PEER_TOPK:
            z = z + jnp.exp(cur - best)
        if r + 1 < n_top:
            cands = [jnp.where(c == cur, -jnp.inf, c) for c in cands]
    tau = 0.5 * (ordered[PEER_TOPK - 1] + ordered[PEER_TOPK])
    inv_z = 1.0 / z
    for h in range(PEER_HEADS):
        s1 = s_scr[0, h]
        s2 = s_scr[1, h]
        s2_ref[h] = s2
        bb_ref[h] = jnp.exp(s2 - v2[0][h:h + 1]) * inv_z[h:h + 1]
        th_ref[h] = tau[h:h + 1] - s1
        aa_ref[h] = jnp.exp(s1 - v1[0][h:h + 1])


def _peer_select(h2t, wqt, keys):
    d, tokens = h2t.shape
    tm = TOKEN_TILE
    shape = jax.ShapeDtypeStruct((PEER_HEADS, PEER_KEYS, tokens), F32)
    blk = pl.BlockSpec((PEER_HEADS, PEER_KEYS, tm), lambda t: (0, 0, t))
    return pl.pallas_call(
        _peer_select_kernel,
        out_shape=(shape, shape, shape, shape),
        grid=(tokens // tm,),
        in_specs=[pl.BlockSpec((d, tm), lambda t: (0, t)),
                  pl.BlockSpec(wqt.shape, lambda t: (0, 0)),
                  pl.BlockSpec(keys.shape, lambda t: (0, 0, 0))],
        out_specs=(blk, blk, blk, blk),
        scratch_shapes=[pltpu.VMEM((2, PEER_HEADS, PEER_KEYS, tm), F32),
                        pltpu.VMEM((2, PEER_TOPK + 1, PEER_HEADS, tm), F32)],
        compiler_params=_params(1),
        name="peer_select",
    )(h2t, wqt, keys)


def _peer_expert_kernel(h2t_ref, u_ref, vt_ref, s2_ref, bb_ref, th_ref, aa_ref,
                        x1_ref, gt2_ref, gf_ref, o_ref, acc_ref, a0_ref, a1_ref, act0_ref, act1_ref):
    e = pl.program_id(1)
    tm = h2t_ref.shape[1]
    rows_per_step = u_ref.shape[0] // PEER_KEYS

    @pl.when(e == 0)
    def _():
        acc_ref[...] = jnp.zeros_like(acc_ref)
        a1_ref[...] = jnp.zeros_like(a1_ref)
        act0_ref[...] = jnp.zeros_like(act0_ref)

    def step(a_new, a_old, act_new, act_old):
        n_units = (tm // LANES) * (PEER_KEYS // GATE_ROWS)
        a_rows = 2 * a_new.shape[0] // n_units
        acc_rows = 2 * acc_ref.shape[0] // n_units
        for c in range(tm // LANES):
            lanes = slice(c * LANES, (c + 1) * LANES)
            for jb in range(PEER_KEYS // GATE_ROWS):
                unit = c * (PEER_KEYS // GATE_ROWS) + jb
                if unit % 2 == 0:
                    chunk = slice((unit // 2) * a_rows, (unit // 2 + 1) * a_rows)
                    a_new[chunk, :] = jnp.dot(u_ref[chunk, :], h2t_ref[...], preferred_element_type=F32)
                else:
                    chunk = slice((unit // 2) * acc_rows, (unit // 2 + 1) * acc_rows)
                    acc_ref[chunk, :] += jnp.dot(vt_ref[chunk, :], act_old[...], preferred_element_type=F32)
                keys = slice(jb * GATE_ROWS, (jb + 1) * GATE_ROWS)
                gates = [jnp.zeros((GATE_ROWS, LANES), F32) for _ in range(rows_per_step)]
                for h in range(PEER_HEADS):
                    s2 = s2_ref[h, keys, lanes]
                    bb = bb_ref[h, keys, lanes]
                    for ii in range(rows_per_step):
                        hit = s2 >= th_ref[h, ii:ii + 1, lanes]
                        gates[ii] = gates[ii] + jnp.where(hit, bb, 0.0) * aa_ref[h, ii:ii + 1, lanes]
                for ii in range(rows_per_step):
                    rows = slice(ii * PEER_KEYS + jb * GATE_ROWS, ii * PEER_KEYS + (jb + 1) * GATE_ROWS)
                    act_new[rows, lanes] = (_gelu_tanh(a_old[rows, lanes]) * gates[ii]).astype(BF16)

    @pl.when(e & 1 == 0)
    def _():
        step(a0_ref, a1_ref, act1_ref, act0_ref)

    @pl.when(e & 1 == 1)
    def _():
        step(a1_ref, a0_ref, act0_ref, act1_ref)

    @pl.when(e == pl.num_programs(1) - 1)
    def _():
        y = x1_ref[...] + gt2_ref[0] * acc_ref[...].T
        o_ref[...] = _rms_norm(y, gf_ref[...])


def _peer_expert(h2t, u, vt, s2, bb, th, aa, x1, gt2, gf, tokens_per_seq):
    d, tokens = h2t.shape
    n_exp = u.shape[0]
    tm = TOKEN_TILE
    eb = EXPERT_TILE
    rows = eb // PEER_KEYS
    tiles_per_seq = tokens_per_seq // tm
    n_tiles = n_exp // eb
    stage = lambda e, lag: jnp.clip(e - lag, 0, n_tiles - 1)
    full = pl.BlockSpec((PEER_HEADS, PEER_KEYS, tm), lambda t, e: (0, 0, t))
    part = pl.BlockSpec((PEER_HEADS, rows, tm), lambda t, e: (0, stage(e, 1), t))
    return pl.pallas_call(
        _peer_expert_kernel,
        out_shape=jax.ShapeDtypeStruct((tokens, d), F32),
        grid=(tokens // tm, n_tiles + 2),
        in_specs=[pl.BlockSpec((d, tm), lambda t, e: (0, t)),
                  pl.BlockSpec((eb, d), lambda t, e: (stage(e, 0), 0)),
                  pl.BlockSpec((d, eb), lambda t, e: (0, stage(e, 2))),
                  full, full, part, part,
                  pl.BlockSpec((tm, d), lambda t, e: (t, 0)),
                  pl.BlockSpec((1, 1, d), lambda t, e: (t // tiles_per_seq, 0, 0)),
                  pl.BlockSpec((1, d), lambda t, e: (0, 0))],
        out_specs=pl.BlockSpec((tm, d), lambda t, e: (t, 0)),
        scratch_shapes=[pltpu.VMEM((d, tm), F32),
                        pltpu.VMEM((eb, tm), F32),
                        pltpu.VMEM((eb, tm), F32),
                        pltpu.VMEM((eb, tm), BF16),
                        pltpu.VMEM((eb, tm), BF16)],
        compiler_params=_params(2),
        name="peer_expert",
    )(h2t, u, vt, s2, bb, th, aa, x1, gt2, gf)


def kernel(x, c, w_ada, b_ada, g_norm1, w_in, w_attn_up, w_pool_mix, b_pool_mix, pool_scale,
           w_pool_up, w_out, g_norm2, w_peer_q, peer_sub_keys, peer_u, peer_v, g_final):
    bsz, s, d = x.shape
    depth = w_ada.shape[0]
    assert depth == 1, "the final norm is fused into the last PEER step of a single layer"
    assert s % TOKEN_TILE == 0 and TOKEN_TILE % MOBA_BLOCK == 0
    assert w_in.shape[-1] == 3 * ATTN_WIDTH + POOL_WIDTH + 2 * d
    for l in range(depth):
        mod = _adaln(c, w_ada[l], b_ada[l])
        sh1, sc1, gt1, sh2, sc2, gt2 = [m.reshape(bsz, 1, d) for m in jnp.split(mod, 6, axis=-1)]
        q, k, vt, p, ga, gb, kmean = _inproj(x, g_norm1[l], sh1, sc1, w_in[l].astype(BF16))
        nb = s // MOBA_BLOCK
        kmean = jnp.pad(kmean.reshape(bsz, nb, ATTN_WIDTH), ((0, 0), (0, -nb % SUBLANES), (0, 0)))
        attn = _moba(q, k, vt, kmean)
        x1, h2t = _mix(x, attn, p, ga, gb,
                       w_attn_up[l].astype(BF16), w_pool_mix[l].astype(BF16),
                       b_pool_mix[l].reshape(len(POOL_WINDOWS), 1, POOL_GROUP_DIM),
                       pool_scale[l].reshape(1, POOL_WIDTH), w_pool_up[l].astype(BF16),
                       w_out[l].astype(BF16), gt1, g_norm2[l].reshape(1, d), sh2, sc2)
        s2, bb, th, aa = _peer_select(h2t, w_peer_q[l].T.astype(BF16), peer_sub_keys[l].astype(BF16))
        y = _peer_expert(h2t, peer_u[l].astype(BF16), peer_v[l].T.astype(BF16),
                         s2, bb, th, aa, x1.reshape(bsz * s, d), gt2, g_final.reshape(1, d), s)
        x = y.reshape(bsz, s, d)
    return x
```

```python
import functools

import jax
import jax.numpy as jnp
import numpy as np
from jax import lax
from jax.experimental import pallas as pl
from jax.experimental.pallas import tpu as pltpu

F32 = jnp.float32
BF16 = jnp.bfloat16

ATTN_HEADS = 8
HEAD_DIM = 64
ATTN_WIDTH = ATTN_HEADS * HEAD_DIM
MOBA_BLOCK = 256
MOBA_TOPK = 3
POOL_WINDOWS = (2, 4, 8, 16)
POOL_GROUP_DIM = 128
POOL_WIDTH = len(POOL_WINDOWS) * POOL_GROUP_DIM
POOL_HALO = 16
PEER_HEADS = 8
PEER_KEYS = 128
PEER_HALF = 128
PEER_TOPK = 16
RMS_EPS = 1e-6
NEG_INF = -1e30
MASKED_SHIFT = 3e38

LANES = 128
SUBLANES = 8
MOBA_HEADS_PER_STEP = 4
VMEM_LIMIT = 56 * 1024 * 1024

TOKEN_TILE = 512
EXPERT_TILE = 1024
PACKED_ROWS = 16
GATE_ROWS = 64
GATE_GROUP = 4
NT_DIMS = (((1,), (1,)), ((), ()))


def _sigmoid(x):
    return 1.0 / (1.0 + jnp.exp(-x))


def _gelu_tanh(x):
    c = np.sqrt(2.0 / np.pi).astype(np.float32)
    return 0.5 * x * (1.0 + jnp.tanh(c * (x + 0.044715 * (x * x * x))))


def _rms_norm(x, gain):
    ms = jnp.mean(x * x, axis=-1, keepdims=True)
    return x * lax.rsqrt(ms + RMS_EPS) * gain


def _params(n_axes):
    return pltpu.CompilerParams(dimension_semantics=("arbitrary",) * n_axes,
                                vmem_limit_bytes=VMEM_LIMIT)


def _adaln_kernel(c_ref, w_ref, b_ref, o_ref):
    c = c_ref[...]
    ca = c * _sigmoid(c)
    o_ref[...] = jnp.dot(ca, w_ref[...], preferred_element_type=F32,
                         precision=lax.Precision.HIGHEST) + b_ref[...]


def _adaln(c, w, b):
    bsz, d = c.shape
    n = w.shape[1]
    tn = 1536
    return pl.pallas_call(
        _adaln_kernel,
        out_shape=jax.ShapeDtypeStruct((bsz, n), F32),
        grid=(n // tn,),
        in_specs=[pl.BlockSpec((bsz, d), lambda j: (0, 0)),
                  pl.BlockSpec((d, tn), lambda j: (0, j)),
                  pl.BlockSpec((1, tn), lambda j: (0, j))],
        out_specs=pl.BlockSpec((bsz, tn), lambda j: (0, j)),
        compiler_params=_params(1),
        name="adaln",
    )(c, w, b.reshape(1, n))


def _inproj_kernel(x_ref, g_ref, sh_ref, sc_ref, w_ref,
                   q_ref, k_ref, vt_ref, p_ref, ga_ref, gb_ref, km_ref):
    d = x_ref.shape[-1]
    a = ATTN_WIDTH
    h = _rms_norm(x_ref[0], g_ref[...]) * (1.0 + sc_ref[0]) + sh_ref[0]
    hb = h.astype(BF16)

    def proj(lo, hi):
        return jnp.dot(hb, w_ref[:, lo:hi], preferred_element_type=F32)

    q_ref[0] = (proj(0, a) * (HEAD_DIM ** -0.5)).astype(BF16)
    k = proj(a, 2 * a)
    k_ref[0] = k.astype(BF16)
    for r in range(k.shape[0] // MOBA_BLOCK):
        km_ref[0, r] = jnp.mean(k[r * MOBA_BLOCK:(r + 1) * MOBA_BLOCK], axis=0, keepdims=True)
    v = proj(2 * a, 3 * a)
    for r in range(v.shape[0] // MOBA_BLOCK):
        vt_ref[0, r] = v[r * MOBA_BLOCK:(r + 1) * MOBA_BLOCK].T.astype(BF16)
    p_ref[0] = proj(3 * a, 3 * a + POOL_WIDTH)
    ga_ref[0] = proj(3 * a + POOL_WIDTH, 3 * a + POOL_WIDTH + d)
    gb_ref[0] = proj(3 * a + POOL_WIDTH + d, 3 * a + POOL_WIDTH + 2 * d)


def _inproj(x, g1, sh1, sc1, w_in):
    bsz, s, d = x.shape
    tm = TOKEN_TILE
    nbt = tm // MOBA_BLOCK
    row = lambda b, t: (b, t, 0)
    per_b = lambda b, t: (b, 0, 0)
    out_shape = (
        jax.ShapeDtypeStruct((bsz, s, ATTN_WIDTH), BF16),
        jax.ShapeDtypeStruct((bsz, s, ATTN_WIDTH), BF16),
        jax.ShapeDtypeStruct((bsz, s // MOBA_BLOCK, ATTN_WIDTH, MOBA_BLOCK), BF16),
        jax.ShapeDtypeStruct((bsz, s, POOL_WIDTH), F32),
        jax.ShapeDtypeStruct((bsz, s, d), F32),
        jax.ShapeDtypeStruct((bsz, s, d), F32),
        jax.ShapeDtypeStruct((bsz, s // MOBA_BLOCK, 1, ATTN_WIDTH), F32),
    )
    return pl.pallas_call(
        _inproj_kernel,
        out_shape=out_shape,
        grid=(bsz, s // tm),
        in_specs=[pl.BlockSpec((1, tm, d), row),
                  pl.BlockSpec((1, d), lambda b, t: (0, 0)),
                  pl.BlockSpec((1, 1, d), per_b),
                  pl.BlockSpec((1, 1, d), per_b),
                  pl.BlockSpec(w_in.shape, lambda b, t: (0, 0))],
        out_specs=(pl.BlockSpec((1, tm, ATTN_WIDTH), row),
                   pl.BlockSpec((1, tm, ATTN_WIDTH), row),
                   pl.BlockSpec((1, nbt, ATTN_WIDTH, MOBA_BLOCK), lambda b, t: (b, t, 0, 0)),
                   pl.BlockSpec((1, tm, POOL_WIDTH), row),
                   pl.BlockSpec((1, tm, d), row),
                   pl.BlockSpec((1, tm, d), row),
                   pl.BlockSpec((1, nbt, 1, ATTN_WIDTH), lambda b, t: (b, t, 0, 0))),
        compiler_params=_params(2),
        name="inproj",
    )(x, g1.reshape(1, d), sh1, sc1, w_in)


def _moba_kernel(slopes_ref, q_ref, k_ref, vt_ref, km_ref, o_ref, sel_ref, srel_ref):
    blk = MOBA_BLOCK
    hg = pl.program_id(1)
    qb = pl.program_id(2)
    n_here = q_ref.shape[-1] // HEAD_DIM
    nbp = km_ref.shape[1]
    rel_t = (lax.broadcasted_iota(jnp.int32, (blk, blk), 0)
             - lax.broadcasted_iota(jnp.int32, (blk, blk), 1)).astype(F32)
    blk_row = lax.broadcasted_iota(jnp.int32, (nbp, blk), 0)
    blk_row_f = blk_row.astype(F32)
    own = pl.multiple_of(qb * blk, blk)
    head_cols = [slice(hh * HEAD_DIM, (hh + 1) * HEAD_DIM) for hh in range(n_here)]
    slopes = [slopes_ref[hg * n_here + hh] for hh in range(n_here)]
    q_heads = [q_ref[0, :, cols] for cols in head_cols]
    state = []
    for hh in range(n_here):
        cols, slope, qh = head_cols[hh], slopes[hh], q_heads[hh]
        srel = slope * rel_t
        srel_ref[hh] = srel

        kmh = km_ref[0, :, cols].astype(BF16)
        gate = lax.dot_general(kmh, qh, NT_DIMS, preferred_element_type=F32)
        gate = jnp.where(blk_row < qb, gate, -jnp.inf)
        sel = jnp.zeros((nbp, blk), F32)
        for _ in range(MOBA_TOPK):
            gmax = jnp.max(gate, axis=0, keepdims=True)
            first = jnp.min(jnp.where(gate == gmax, blk_row_f, float(nbp)), axis=0, keepdims=True)
            first = jnp.where(gmax == -jnp.inf, float(nbp), first)
            pick = blk_row_f == first
            sel = jnp.where(pick, 1.0, sel)
            gate = jnp.where(pick, -jnp.inf, gate)
        for r in range(nbp):
            sel_ref[hh, r] = sel[r:r + 1, :]

        k_own = k_ref[0, pl.ds(own, blk), cols]
        t = lax.dot_general(k_own, qh, NT_DIMS, preferred_element_type=F32) + srel
        t = jnp.where(rel_t <= 0.0, t, NEG_INF)
        m0 = jnp.max(t, axis=0, keepdims=True)
        p = jnp.exp(t - m0)
        l0 = jnp.sum(p, axis=0, keepdims=True)
        acc0 = jnp.dot(vt_ref[0, qb, cols, :], p.astype(BF16), preferred_element_type=F32)
        state += [m0, l0, acc0]

    def past_block(j, carry):
        start = pl.multiple_of(j * blk, blk)
        scores = [lax.dot_general(k_ref[0, pl.ds(start, blk), cols], q_heads[hh], NT_DIMS,
                                  preferred_element_type=F32) for hh, cols in enumerate(head_cols)]
        probs, scaled = [], []
        for hh in range(n_here):
            m, l, _ = carry[3 * hh:3 * hh + 3]
            t = scores[hh] + srel_ref[hh]
            block_bias = slopes[hh] * ((qb - j) * blk).astype(F32)
            sel_j = sel_ref[hh, j] > 0.5
            mb = jnp.max(t, axis=0, keepdims=True) - block_bias
            m_new = jnp.maximum(m, jnp.where(sel_j, mb, -jnp.inf))
            shift = jnp.where(sel_j, m_new + block_bias, MASKED_SHIFT)
            p = jnp.exp(t - shift)
            alpha = jnp.exp(m - m_new)
            probs.append(p.astype(BF16))
            scaled.append((m_new, alpha * l + jnp.sum(p, axis=0, keepdims=True), alpha))
        nxt = []
        for hh, cols in enumerate(head_cols):
            m_new, l_new, alpha = scaled[hh]
            pv = jnp.dot(vt_ref[0, j, cols, :], probs[hh], preferred_element_type=F32)
            nxt += [m_new, l_new, alpha * carry[3 * hh + 2] + pv]
        return tuple(nxt)

    final = lax.fori_loop(0, qb, past_block, tuple(state))
    out_t = jnp.concatenate([final[3 * hh + 2] / final[3 * hh + 1] for hh in range(n_here)], axis=0)
    o_ref[0] = out_t.T.astype(o_ref.dtype)


def _moba(q, k, vt, kmean):
    bsz, s, a = q.shape
    blk = MOBA_BLOCK
    nb = s // blk
    nbp = kmean.shape[1]
    hw = MOBA_HEADS_PER_STEP * HEAD_DIM
    slopes = jnp.asarray(2.0 ** (-8.0 * (np.arange(ATTN_HEADS) + 1) / ATTN_HEADS), dtype=F32)
    return pl.pallas_call(
        _moba_kernel,
        out_shape=jax.ShapeDtypeStruct((bsz, s, a), BF16),
        grid=(bsz, a // hw, nb),
        in_specs=[pl.BlockSpec(memory_space=pltpu.SMEM),
                  pl.BlockSpec((1, blk, hw), lambda b, h, i: (b, i, h)),
                  pl.BlockSpec((1, s, hw), lambda b, h, i: (b, 0, h)),
                  pl.BlockSpec((1, nb, hw, blk), lambda b, h, i: (b, 0, h, 0)),
                  pl.BlockSpec((1, nbp, hw), lambda b, h, i: (b, 0, h))],
        out_specs=pl.BlockSpec((1, blk, hw), lambda b, h, i: (b, i, h)),
        scratch_shapes=[pltpu.VMEM((MOBA_HEADS_PER_STEP, nbp, 1, blk), F32),
                        pltpu.VMEM((MOBA_HEADS_PER_STEP, blk, blk), F32)],
        compiler_params=_params(3),
        name="moba",
    )(slopes, q, k, vt, kmean)


def _mix_kernel(x_ref, attn_ref, p_ref, pprev_ref, ga_ref, gb_ref,
                wau_ref, wmix_ref, bmix_ref, psc_ref, wpu_ref, wout_ref,
                gt1_ref, g2_ref, sh2_ref, sc2_ref,
                x1_ref, h2t_ref, hist_ref):
    tm = x_ref.shape[1]
    t = pl.program_id(1)
    hist_ref[0:POOL_HALO, :] = jnp.where(t == 0, 0.0, pprev_ref[0])
    hist_ref[POOL_HALO:, :] = p_ref[0]
    pos = t * tm + lax.broadcasted_iota(jnp.int32, (tm, 1), 0)
    mixed = []
    for g, w in enumerate(POOL_WINDOWS):
        cols = slice(g * POOL_GROUP_DIM, (g + 1) * POOL_GROUP_DIM)
        win = hist_ref[POOL_HALO - (w - 1):POOL_HALO - (w - 1) + tm, cols]
        for dlt in range(w - 2, -1, -1):
            win = win + hist_ref[POOL_HALO - dlt:POOL_HALO - dlt + tm, cols]
        cnt = jnp.minimum(pos + 1, w).astype(F32)
        pooled = win / cnt - hist_ref[POOL_HALO:, cols]
        mg = jnp.dot(pooled.astype(BF16), wmix_ref[g], preferred_element_type=F32) + bmix_ref[g]
        mixed.append(mg * psc_ref[:, cols])
    pool = jnp.concatenate(mixed, axis=-1).astype(BF16)
    up_a = jnp.dot(attn_ref[0], wau_ref[...], preferred_element_type=F32)
    up_b = jnp.dot(pool, wpu_ref[...], preferred_element_type=F32)
    merged = _sigmoid(ga_ref[0]) * up_a + _sigmoid(gb_ref[0]) * up_b
    x1 = x_ref[0] + gt1_ref[0] * jnp.dot(merged.astype(BF16), wout_ref[...], preferred_element_type=F32)
    x1_ref[0] = x1
    h2 = _rms_norm(x1, g2_ref[...]) * (1.0 + sc2_ref[0]) + sh2_ref[0]
    h2t_ref[...] = h2.T.astype(BF16)


def _mix(x, attn, p, ga, gb, wau, wmix, bmix, psc, wpu, wout, gt1, g2, sh2, sc2):
    bsz, s, d = x.shape
    tm = TOKEN_TILE
    nt = s // tm
    row = lambda b, t: (b, t, 0)
    per_b = lambda b, t: (b, 0, 0)
    whole2 = lambda b, t: (0, 0)
    whole3 = lambda b, t: (0, 0, 0)
    halo_blocks = tm // POOL_HALO
    return pl.pallas_call(
        _mix_kernel,
        out_shape=(jax.ShapeDtypeStruct((bsz, s, d), F32),
                   jax.ShapeDtypeStruct((d, bsz * s), BF16)),
        grid=(bsz, nt),
        in_specs=[pl.BlockSpec((1, tm, d), row),
                  pl.BlockSpec((1, tm, ATTN_WIDTH), row),
                  pl.BlockSpec((1, tm, POOL_WIDTH), row),
                  pl.BlockSpec((1, POOL_HALO, POOL_WIDTH),
                               lambda b, t: (b, jnp.maximum(t * halo_blocks - 1, 0), 0)),
                  pl.BlockSpec((1, tm, d), row),
                  pl.BlockSpec((1, tm, d), row),
                  pl.BlockSpec(wau.shape, whole2),
                  pl.BlockSpec(wmix.shape, whole3),
                  pl.BlockSpec(bmix.shape, whole3),
                  pl.BlockSpec(psc.shape, whole2),
                  pl.BlockSpec(wpu.shape, whole2),
                  pl.BlockSpec(wout.shape, whole2),
                  pl.BlockSpec((1, 1, d), per_b),
                  pl.BlockSpec((1, d), whole2),
                  pl.BlockSpec((1, 1, d), per_b),
                  pl.BlockSpec((1, 1, d), per_b)],
        out_specs=(pl.BlockSpec((1, tm, d), row),
                   pl.BlockSpec((d, tm), lambda b, t: (0, b * nt + t))),
        scratch_shapes=[pltpu.VMEM((tm + POOL_HALO, POOL_WIDTH), F32)],
        compiler_params=_params(2),
        name="mix",
    )(x, attn, p, p, ga, gb, wau, wmix, bmix, psc, wpu, wout, gt1, g2, sh2, sc2)


def _staircase(n):
    return [(a, b) for a in range(n) for b in range(n) if (a + 1) * (b + 1) <= n]


def _peer_select_kernel(h2t_ref, wqt_ref, keys_ref,
                        r2_ref, bb_ref, cnt_ref, aa_ref,
                        s_scr, top_scr):
    n_top = PEER_TOPK + 1
    h2t = h2t_ref[...]
    qdim = 2 * PEER_HALF
    for h in range(PEER_HEADS):
        qt = jnp.dot(wqt_ref[h * qdim:(h + 1) * qdim, :], h2t, preferred_element_type=F32).astype(BF16)
        for part in range(2):
            st = jnp.dot(keys_ref[part], qt[part * PEER_HALF:(part + 1) * PEER_HALF],
                         preferred_element_type=F32)
            s_scr[part, h] = st

            def extract(r, w, part=part, h=h):
                top = jnp.max(w, axis=0, keepdims=True)
                top_scr[part, pl.ds(r, 1), h:h + 1, :] = top[None]
                return jnp.where(w == top, -jnp.inf, w)

            lax.fori_loop(0, n_top, extract, st)

    v1 = [top_scr[0, r] for r in range(n_top)]
    v2 = [top_scr[1, r] for r in range(n_top)]
    sums = {ab: v1[ab[0]] + v2[ab[1]] for ab in _staircase(n_top)}
    cands = list(sums.values())
    best = None
    z = None
    ordered = []
    for r in range(n_top):
        cur = functools.reduce(jnp.maximum, cands)
        ordered.append(cur)
        if r == 0:
            best = cur
            z = jnp.ones_like(cur)
        elif r < PEER_TOPK:
            z = z + jnp.exp(cur - best)
        if r + 1 < n_top:
            cands = [jnp.where(c == cur, -jnp.inf, c) for c in cands]
    tau = 0.5 * (ordered[PEER_TOPK - 1] + ordered[PEER_TOPK])
    inv_z = 1.0 / z
    partners = []
    for a in range(PEER_TOPK):
        hits = [jnp.where(sums[(a, b)] >= tau, 1.0, 0.0)
                for b in range(PEER_TOPK) if (a + 1) * (b + 1) <= PEER_TOPK]
        partners.append(functools.reduce(jnp.add, hits))
    for h in range(PEER_HEADS):
        s1 = s_scr[0, h]
        s2 = s_scr[1, h]
        cnt = jnp.zeros_like(s1)
        for a in range(PEER_TOPK):
            cnt = jnp.where(s1 == v1[a][h:h + 1], partners[a][h:h + 1], cnt)
        rank2 = jnp.zeros_like(s2)
        for r in range(n_top):
            rank2 = rank2 + jnp.where(v2[r][h:h + 1] > s2, 1.0, 0.0)
        cnt_ref[h] = cnt
        aa_ref[h] = jnp.exp(s1 - v1[0][h:h + 1])
        r2_ref[h] = rank2
        bb_ref[h] = jnp.exp(s2 - v2[0][h:h + 1]) * inv_z[h:h + 1]


def _peer_select(h2t, wqt, keys):
    d, tokens = h2t.shape
    tm = TOKEN_TILE
    wide = jax.ShapeDtypeStruct((PEER_HEADS, PEER_KEYS, tokens), F32)
    blk = pl.BlockSpec((PEER_HEADS, PEER_KEYS, tm), lambda t: (0, 0, t))
    return pl.pallas_call(
        _peer_select_kernel,
        out_shape=(wide, wide, wide, wide),
        grid=(tokens // tm,),
        in_specs=[pl.BlockSpec((d, tm), lambda t: (0, t)),
                  pl.BlockSpec(wqt.shape, lambda t: (0, 0)),
                  pl.BlockSpec(keys.shape, lambda t: (0, 0, 0))],
        out_specs=(blk, blk, blk, blk),
        scratch_shapes=[pltpu.VMEM((2, PEER_HEADS, PEER_KEYS, tm), F32),
                        pltpu.VMEM((2, PEER_TOPK + 1, PEER_HEADS, tm), F32)],
        compiler_params=_params(1),
        name="peer_select",
    )(h2t, wqt, keys)


def _peer_expert_kernel(h2t_ref, u_ref, vt_ref, r2_ref, bb_ref, cnt_ref, aa_ref,
                        x1_ref, gt2_ref, gf_ref, o_ref, acc_ref, a_ref, act_ref):
    e = pl.program_id(1)
    tm = h2t_ref.shape[1]
    rows_per_step = u_ref.shape[0] // PEER_KEYS
    pk = PACKED_ROWS

    @pl.when(e == 0)
    def _():
        acc_ref[...] = jnp.zeros_like(acc_ref)

    a_ref[...] = jnp.dot(u_ref[...], h2t_ref[...], preferred_element_type=F32)
    zero = jnp.zeros((pk, LANES), BF16)
    key_blocks = PEER_KEYS // GATE_ROWS
    row_groups = rows_per_step // GATE_GROUP
    tiles_per_block = GATE_ROWS // pk

    def gate_block(lanes, key0, row0):
        key_tiles = [slice(key0 + k * pk, key0 + (k + 1) * pk) for k in range(tiles_per_block)]
        gates = [[zero for _ in key_tiles] for _ in range(GATE_GROUP)]
        for h in range(PEER_HEADS):
            r2 = [r2_ref[h, kt, lanes].astype(BF16) for kt in key_tiles]
            bb = [bb_ref[h, kt, lanes].astype(BF16) for kt in key_tiles]
            for g in range(GATE_GROUP):
                ii = row0 + g
                cnt = jnp.broadcast_to(cnt_ref[h, ii:ii + 1, lanes], (pk, LANES)).astype(BF16)
                aa = jnp.broadcast_to(aa_ref[h, ii:ii + 1, lanes], (pk, LANES)).astype(BF16)
                for k in range(tiles_per_block):
                    gates[g][k] = gates[g][k] + jnp.where(r2[k] < cnt, bb[k], zero) * aa
        for g in range(GATE_GROUP):
            for k, kt in enumerate(key_tiles):
                rows = slice((row0 + g) * PEER_KEYS + kt.start, (row0 + g) * PEER_KEYS + kt.stop)
                act_ref[rows, lanes] = (_gelu_tanh(a_ref[rows, lanes])
                                        * gates[g][k].astype(F32)).astype(BF16)

    for c in range(tm // LANES):
        for jb in range(key_blocks):
            for ig in range(row_groups):
                gate_block(slice(c * LANES, (c + 1) * LANES), jb * GATE_ROWS, ig * GATE_GROUP)
    acc_ref[...] += jnp.dot(vt_ref[...], act_ref[...], preferred_element_type=F32)

    @pl.when(e == pl.num_programs(1) - 1)
    def _():
        y = x1_ref[...] + gt2_ref[0] * acc_ref[...].T
        o_ref[...] = _rms_norm(y, gf_ref[...])


def _peer_expert(h2t, u, vt, r2, bb, cnt, aa, x1, gt2, gf, tokens_per_seq):
    d, tokens = h2t.shape
    n_exp = u.shape[0]
    tm = TOKEN_TILE
    eb = EXPERT_TILE
    rows = eb // PEER_KEYS
    tiles_per_seq = tokens_per_seq // tm
    full = pl.BlockSpec((PEER_HEADS, PEER_KEYS, tm), lambda t, e: (0, 0, t))
    part = pl.BlockSpec((PEER_HEADS, rows, tm), lambda t, e: (0, e, t))
    return pl.pallas_call(
        _peer_expert_kernel,
        out_shape=jax.ShapeDtypeStruct((tokens, d), F32),
        grid=(tokens // tm, n_exp // eb),
        in_specs=[pl.BlockSpec((d, tm), lambda t, e: (0, t)),
                  pl.BlockSpec((eb, d), lambda t, e: (e, 0)),
                  pl.BlockSpec((d, eb), lambda t, e: (0, e)),
                  full, full, part, part,
                  pl.BlockSpec((tm, d), lambda t, e: (t, 0)),
                  pl.BlockSpec((1, 1, d), lambda t, e: (t // tiles_per_seq, 0, 0)),
                  pl.BlockSpec((1, d), lambda t, e: (0, 0))],
        out_specs=pl.BlockSpec((tm, d), lambda t, e: (t, 0)),
        scratch_shapes=[pltpu.VMEM((d, tm), F32),
                        pltpu.VMEM((eb, tm), F32),
                        pltpu.VMEM((eb, tm), BF16)],
        compiler_params=_params(2),
        name="peer_expert",
    )(h2t, u, vt, r2, bb, cnt, aa, x1, gt2, gf)


def kernel(x, c, w_ada, b_ada, g_norm1, w_in, w_attn_up, w_pool_mix, b_pool_mix, pool_scale,
           w_pool_up, w_out, g_norm2, w_peer_q, peer_sub_keys, peer_u, peer_v, g_final):
    bsz, s, d = x.shape
    depth = w_ada.shape[0]
    assert depth == 1, "the final norm is fused into the last PEER step of a single layer"
    assert s % TOKEN_TILE == 0 and TOKEN_TILE % MOBA_BLOCK == 0
    assert w_in.shape[-1] == 3 * ATTN_WIDTH + POOL_WIDTH + 2 * d
    for l in range(depth):
        mod = _adaln(c, w_ada[l], b_ada[l])
        sh1, sc1, gt1, sh2, sc2, gt2 = [m.reshape(bsz, 1, d) for m in jnp.split(mod, 6, axis=-1)]
        q, k, vt, p, ga, gb, kmean = _inproj(x, g_norm1[l], sh1, sc1, w_in[l].astype(BF16))
        nb = s // MOBA_BLOCK
        kmean = jnp.pad(kmean.reshape(bsz, nb, ATTN_WIDTH), ((0, 0), (0, -nb % SUBLANES), (0, 0)))
        attn = _moba(q, k, vt, kmean)
        x1, h2t = _mix(x, attn, p, ga, gb,
                       w_attn_up[l].astype(BF16), w_pool_mix[l].astype(BF16),
                       b_pool_mix[l].reshape(len(POOL_WINDOWS), 1, POOL_GROUP_DIM),
                       pool_scale[l].reshape(1, POOL_WIDTH), w_pool_up[l].astype(BF16),
                       w_out[l].astype(BF16), gt1, g_norm2[l].reshape(1, d), sh2, sc2)
        r2, bb, cnt, aa = _peer_select(h2t, w_peer_q[l].T.astype(BF16), peer_sub_keys[l].astype(BF16))
        y = _peer_expert(h2t, peer_u[l].astype(BF16), peer_v[l].T.astype(BF16),
                         r2, bb, cnt, aa, x1.reshape(bsz * s, d), gt2, g_final.reshape(1, d), s)
        x = y.reshape(bsz, s, d)
    return x
```

```python
import functools

import jax
import jax.numpy as jnp
import numpy as np
from jax import lax
from jax.experimental import pallas as pl
from jax.experimental.pallas import tpu as pltpu

F32 = jnp.float32
BF16 = jnp.bfloat16

ATTN_HEADS = 8
HEAD_DIM = 64
ATTN_WIDTH = ATTN_HEADS * HEAD_DIM
MOBA_BLOCK = 256
MOBA_TOPK = 3
POOL_WINDOWS = (2, 4, 8, 16)
POOL_GROUP_DIM = 128
POOL_WIDTH = len(POOL_WINDOWS) * POOL_GROUP_DIM
POOL_HALO = 16
PEER_HEADS = 8
PEER_KEYS = 128
PEER_HALF = 128
PEER_TOPK = 16
RMS_EPS = 1e-6
NEG_INF = -1e30
MASKED_SHIFT = 3e38

LANES = 128
SUBLANES = 8
MOBA_HEADS_PER_STEP = 8
VMEM_LIMIT = 56 * 1024 * 1024

TOKEN_TILE = 512
EXPERT_TILE = 2048
PACKED_ROWS = 16
GATE_ROWS = 64
GATE_GROUP = 8
NT_DIMS = (((1,), (1,)), ((), ()))


def _sigmoid(x):
    return 1.0 / (1.0 + jnp.exp(-x))


def _gelu_tanh(x):
    c = float(np.sqrt(2.0 / np.pi))
    return 0.5 * x * (1.0 + jnp.tanh(c * (x + 0.044715 * (x * x * x))))


def _rms_norm(x, gain):
    ms = jnp.mean(x * x, axis=-1, keepdims=True)
    return x * lax.rsqrt(ms + RMS_EPS) * gain


def _params(n_axes):
    return pltpu.CompilerParams(dimension_semantics=("arbitrary",) * n_axes,
                                vmem_limit_bytes=VMEM_LIMIT)


def _adaln_kernel(c_ref, w_ref, b_ref, o_ref):
    c = c_ref[...]
    ca = c * _sigmoid(c)
    o_ref[...] = jnp.dot(ca, w_ref[...], preferred_element_type=F32,
                         precision=lax.Precision.HIGHEST) + b_ref[...]


def _adaln(c, w, b):
    bsz, d = c.shape
    n = w.shape[1]
    tn = 1536
    return pl.pallas_call(
        _adaln_kernel,
        out_shape=jax.ShapeDtypeStruct((bsz, n), F32),
        grid=(n // tn,),
        in_specs=[pl.BlockSpec((bsz, d), lambda j: (0, 0)),
                  pl.BlockSpec((d, tn), lambda j: (0, j)),
                  pl.BlockSpec((1, tn), lambda j: (0, j))],
        out_specs=pl.BlockSpec((bsz, tn), lambda j: (0, j)),
        compiler_params=_params(1),
        name="adaln",
    )(c, w, b.reshape(1, n))


def _inproj_kernel(x_ref, g_ref, sh_ref, sc_ref, w_ref,
                   q_ref, k_ref, vt_ref, p_ref, ga_ref, gb_ref, km_ref):
    d = x_ref.shape[-1]
    a = ATTN_WIDTH
    h = _rms_norm(x_ref[0], g_ref[...]) * (1.0 + sc_ref[0]) + sh_ref[0]
    hb = h.astype(BF16)

    def proj(lo, hi):
        return jnp.dot(hb, w_ref[:, lo:hi], preferred_element_type=F32)

    q_ref[0] = (proj(0, a) * (HEAD_DIM ** -0.5)).astype(BF16)
    k = proj(a, 2 * a)
    k_ref[0] = k.astype(BF16)
    for r in range(k.shape[0] // MOBA_BLOCK):
        km_ref[0, r] = jnp.mean(k[r * MOBA_BLOCK:(r + 1) * MOBA_BLOCK], axis=0, keepdims=True)
    v = proj(2 * a, 3 * a)
    for r in range(v.shape[0] // MOBA_BLOCK):
        vt_ref[0, r] = v[r * MOBA_BLOCK:(r + 1) * MOBA_BLOCK].T.astype(BF16)
    p_ref[0] = proj(3 * a, 3 * a + POOL_WIDTH)
    ga_ref[0] = proj(3 * a + POOL_WIDTH, 3 * a + POOL_WIDTH + d)
    gb_ref[0] = proj(3 * a + POOL_WIDTH + d, 3 * a + POOL_WIDTH + 2 * d)


def _inproj(x, g1, sh1, sc1, w_in):
    bsz, s, d = x.shape
    tm = TOKEN_TILE
    nbt = tm // MOBA_BLOCK
    row = lambda b, t: (b, t, 0)
    per_b = lambda b, t: (b, 0, 0)
    out_shape = (
        jax.ShapeDtypeStruct((bsz, s, ATTN_WIDTH), BF16),
        jax.ShapeDtypeStruct((bsz, s, ATTN_WIDTH), BF16),
        jax.ShapeDtypeStruct((bsz, s // MOBA_BLOCK, ATTN_WIDTH, MOBA_BLOCK), BF16),
        jax.ShapeDtypeStruct((bsz, s, POOL_WIDTH), F32),
        jax.ShapeDtypeStruct((bsz, s, d), F32),
        jax.ShapeDtypeStruct((bsz, s, d), F32),
        jax.ShapeDtypeStruct((bsz, s // MOBA_BLOCK, 1, ATTN_WIDTH), F32),
    )
    return pl.pallas_call(
        _inproj_kernel,
        out_shape=out_shape,
        grid=(bsz, s // tm),
        in_specs=[pl.BlockSpec((1, tm, d), row),
                  pl.BlockSpec((1, d), lambda b, t: (0, 0)),
                  pl.BlockSpec((1, 1, d), per_b),
                  pl.BlockSpec((1, 1, d), per_b),
                  pl.BlockSpec(w_in.shape, lambda b, t: (0, 0))],
        out_specs=(pl.BlockSpec((1, tm, ATTN_WIDTH), row),
                   pl.BlockSpec((1, tm, ATTN_WIDTH), row),
                   pl.BlockSpec((1, nbt, ATTN_WIDTH, MOBA_BLOCK), lambda b, t: (b, t, 0, 0)),
                   pl.BlockSpec((1, tm, POOL_WIDTH), row),
                   pl.BlockSpec((1, tm, d), row),
                   pl.BlockSpec((1, tm, d), row),
                   pl.BlockSpec((1, nbt, 1, ATTN_WIDTH), lambda b, t: (b, t, 0, 0))),
        compiler_params=_params(2),
        name="inproj",
    )(x, g1.reshape(1, d), sh1, sc1, w_in)


def _moba_kernel(slopes_ref, q_ref, k_ref, vt_ref, km_ref, o_ref, sel_ref, srel_ref):
    blk = MOBA_BLOCK
    hg = pl.program_id(1)
    qb = pl.program_id(2)
    n_here = q_ref.shape[-1] // HEAD_DIM
    nbp = km_ref.shape[1]
    rel_t = (lax.broadcasted_iota(jnp.int32, (blk, blk), 0)
             - lax.broadcasted_iota(jnp.int32, (blk, blk), 1)).astype(F32)
    blk_row = lax.broadcasted_iota(jnp.int32, (nbp, blk), 0)
    blk_row_f = blk_row.astype(F32)
    own = pl.multiple_of(qb * blk, blk)
    head_cols = [slice(hh * HEAD_DIM, (hh + 1) * HEAD_DIM) for hh in range(n_here)]
    slopes = [slopes_ref[hg * n_here + hh] for hh in range(n_here)]
    q_heads = [q_ref[0, :, cols] for cols in head_cols]
    state = []
    for hh in range(n_here):
        cols, slope, qh = head_cols[hh], slopes[hh], q_heads[hh]
        srel = slope * rel_t
        srel_ref[hh] = srel

        kmh = km_ref[0, :, cols].astype(BF16)
        gate = lax.dot_general(kmh, qh, NT_DIMS, preferred_element_type=F32)
        gate = jnp.where(blk_row < qb, gate, -jnp.inf)
        sel = jnp.zeros((nbp, blk), F32)
        for _ in range(MOBA_TOPK):
            gmax = jnp.max(gate, axis=0, keepdims=True)
            first = jnp.min(jnp.where(gate == gmax, blk_row_f, float(nbp)), axis=0, keepdims=True)
            first = jnp.where(gmax == -jnp.inf, float(nbp), first)
            pick = blk_row_f == first
            sel = jnp.where(pick, 1.0, sel)
            gate = jnp.where(pick, -jnp.inf, gate)
        for r in range(nbp):
            sel_ref[hh, r] = sel[r:r + 1, :]

        k_own = k_ref[0, pl.ds(own, blk), cols]
        t = lax.dot_general(k_own, qh, NT_DIMS, preferred_element_type=F32) + srel
        t = jnp.where(rel_t <= 0.0, t, NEG_INF)
        m0 = jnp.max(t, axis=0, keepdims=True)
        p = jnp.exp(t - m0)
        l0 = jnp.sum(p, axis=0, keepdims=True)
        acc0 = jnp.dot(vt_ref[0, qb, cols, :], p.astype(BF16), preferred_element_type=F32)
        state += [m0, l0, acc0]

    def past_block(j, carry):
        start = pl.multiple_of(j * blk, blk)
        scores = [lax.dot_general(k_ref[0, pl.ds(start, blk), cols], q_heads[hh], NT_DIMS,
                                  preferred_element_type=F32) for hh, cols in enumerate(head_cols)]
        probs, scaled = [], []
        for hh in range(n_here):
            m, l, _ = carry[3 * hh:3 * hh + 3]
            t = scores[hh] + srel_ref[hh]
            block_bias = slopes[hh] * ((qb - j) * blk).astype(F32)
            sel_j = sel_ref[hh, j] > 0.5
            mb = jnp.max(t, axis=0, keepdims=True) - block_bias
            m_new = jnp.maximum(m, jnp.where(sel_j, mb, -jnp.inf))
            shift = jnp.where(sel_j, m_new + block_bias, MASKED_SHIFT)
            p = jnp.exp(t - shift)
            alpha = jnp.exp(m - m_new)
            probs.append(p.astype(BF16))
            scaled.append((m_new, alpha * l + jnp.sum(p, axis=0, keepdims=True), alpha))
        nxt = []
        for hh, cols in enumerate(head_cols):
            m_new, l_new, alpha = scaled[hh]
            pv = jnp.dot(vt_ref[0, j, cols, :], probs[hh], preferred_element_type=F32)
            nxt += [m_new, l_new, alpha * carry[3 * hh + 2] + pv]
        return tuple(nxt)

    final = lax.fori_loop(0, qb, past_block, tuple(state))
    out_t = jnp.concatenate([final[3 * hh + 2] / final[3 * hh + 1] for hh in range(n_here)], axis=0)
    o_ref[0] = out_t.T.astype(o_ref.dtype)


def _moba(q, k, vt, kmean):
    bsz, s, a = q.shape
    blk = MOBA_BLOCK
    nb = s // blk
    nbp = kmean.shape[1]
    hw = MOBA_HEADS_PER_STEP * HEAD_DIM
    slopes = jnp.asarray(2.0 ** (-8.0 * (np.arange(ATTN_HEADS) + 1) / ATTN_HEADS), dtype=F32)
    return pl.pallas_call(
        _moba_kernel,
        out_shape=jax.ShapeDtypeStruct((bsz, s, a), BF16),
        grid=(bsz, a // hw, nb),
        in_specs=[pl.BlockSpec(memory_space=pltpu.SMEM),
                  pl.BlockSpec((1, blk, hw), lambda b, h, i: (b, i, h)),
                  pl.BlockSpec((1, s, hw), lambda b, h, i: (b, 0, h)),
                  pl.BlockSpec((1, nb, hw, blk), lambda b, h, i: (b, 0, h, 0)),
                  pl.BlockSpec((1, nbp, hw), lambda b, h, i: (b, 0, h))],
        out_specs=pl.BlockSpec((1, blk, hw), lambda b, h, i: (b, i, h)),
        scratch_shapes=[pltpu.VMEM((MOBA_HEADS_PER_STEP, nbp, 1, blk), F32),
                        pltpu.VMEM((MOBA_HEADS_PER_STEP, blk, blk), F32)],
        compiler_params=_params(3),
        name="moba",
    )(slopes, q, k, vt, kmean)


def _mix_kernel(x_ref, attn_ref, p_ref, pprev_ref, ga_ref, gb_ref,
                wau_ref, wmix_ref, bmix_ref, psc_ref, wpu_ref, wout_ref,
                gt1_ref, g2_ref, sh2_ref, sc2_ref,
                x1_ref, h2t_ref, hist_ref):
    tm = x_ref.shape[1]
    t = pl.program_id(1)
    hist_ref[0:POOL_HALO, :] = jnp.where(t == 0, 0.0, pprev_ref[0])
    hist_ref[POOL_HALO:, :] = p_ref[0]
    pos = t * tm + lax.broadcasted_iota(jnp.int32, (tm, 1), 0)
    mixed = []
    for g, w in enumerate(POOL_WINDOWS):
        cols = slice(g * POOL_GROUP_DIM, (g + 1) * POOL_GROUP_DIM)
        win = hist_ref[POOL_HALO - (w - 1):POOL_HALO - (w - 1) + tm, cols]
        for dlt in range(w - 2, -1, -1):
            win = win + hist_ref[POOL_HALO - dlt:POOL_HALO - dlt + tm, cols]
        cnt = jnp.minimum(pos + 1, w).astype(F32)
        pooled = win / cnt - hist_ref[POOL_HALO:, cols]
        mg = jnp.dot(pooled.astype(BF16), wmix_ref[g], preferred_element_type=F32) + bmix_ref[g]
        mixed.append(mg * psc_ref[:, cols])
    pool = jnp.concatenate(mixed, axis=-1).astype(BF16)
    up_a = jnp.dot(attn_ref[0], wau_ref[...], preferred_element_type=F32)
    up_b = jnp.dot(pool, wpu_ref[...], preferred_element_type=F32)
    merged = _sigmoid(ga_ref[0]) * up_a + _sigmoid(gb_ref[0]) * up_b
    x1 = x_ref[0] + gt1_ref[0] * jnp.dot(merged.astype(BF16), wout_ref[...], preferred_element_type=F32)
    x1_ref[0] = x1
    h2 = _rms_norm(x1, g2_ref[...]) * (1.0 + sc2_ref[0]) + sh2_ref[0]
    h2t_ref[...] = h2.T.astype(BF16)


def _mix(x, attn, p, ga, gb, wau, wmix, bmix, psc, wpu, wout, gt1, g2, sh2, sc2):
    bsz, s, d = x.shape
    tm = TOKEN_TILE
    nt = s // tm
    row = lambda b, t: (b, t, 0)
    per_b = lambda b, t: (b, 0, 0)
    whole2 = lambda b, t: (0, 0)
    whole3 = lambda b, t: (0, 0, 0)
    halo_blocks = tm // POOL_HALO
    return pl.pallas_call(
        _mix_kernel,
        out_shape=(jax.ShapeDtypeStruct((bsz, s, d), F32),
                   jax.ShapeDtypeStruct((d, bsz * s), BF16)),
        grid=(bsz, nt),
        in_specs=[pl.BlockSpec((1, tm, d), row),
                  pl.BlockSpec((1, tm, ATTN_WIDTH), row),
                  pl.BlockSpec((1, tm, POOL_WIDTH), row),
                  pl.BlockSpec((1, POOL_HALO, POOL_WIDTH),
                               lambda b, t: (b, jnp.maximum(t * halo_blocks - 1, 0), 0)),
                  pl.BlockSpec((1, tm, d), row),
                  pl.BlockSpec((1, tm, d), row),
                  pl.BlockSpec(wau.shape, whole2),
                  pl.BlockSpec(wmix.shape, whole3),
                  pl.BlockSpec(bmix.shape, whole3),
                  pl.BlockSpec(psc.shape, whole2),
                  pl.BlockSpec(wpu.shape, whole2),
                  pl.BlockSpec(wout.shape, whole2),
                  pl.BlockSpec((1, 1, d), per_b),
                  pl.BlockSpec((1, d), whole2),
                  pl.BlockSpec((1, 1, d), per_b),
                  pl.BlockSpec((1, 1, d), per_b)],
        out_specs=(pl.BlockSpec((1, tm, d), row),
                   pl.BlockSpec((d, tm), lambda b, t: (0, b * nt + t))),
        scratch_shapes=[pltpu.VMEM((tm + POOL_HALO, POOL_WIDTH), F32)],
        compiler_params=_params(2),
        name="mix",
    )(x, attn, p, p, ga, gb, wau, wmix, bmix, psc, wpu, wout, gt1, g2, sh2, sc2)


def _staircase(n):
    return [(a, b) for a in range(n) for b in range(n) if (a + 1) * (b + 1) <= n]


def _peer_select_kernel(h2t_ref, wqt_ref, keys_ref,
                        r2_ref, bb_ref, cnt_ref, aa_ref,
                        s_scr, top_scr):
    n_top = PEER_TOPK + 1
    h2t = h2t_ref[...]
    qdim = 2 * PEER_HALF
    for h in range(PEER_HEADS):
        qt = jnp.dot(wqt_ref[h * qdim:(h + 1) * qdim, :], h2t, preferred_element_type=F32).astype(BF16)
        for part in range(2):
            st = jnp.dot(keys_ref[part], qt[part * PEER_HALF:(part + 1) * PEER_HALF],
                         preferred_element_type=F32)
            s_scr[part, h] = st

    def extract(r, above):
        nxt = []
        for part in range(2):
            tops = []
            for h in range(PEER_HEADS):
                w = s_scr[part, h]
                tops.append(jnp.max(jnp.where(w < above[part][h:h + 1], w, -jnp.inf), axis=0, keepdims=True))
            slab = jnp.concatenate(tops, axis=0)
            top_scr[part, pl.ds(r, 1)] = slab[None]
            nxt.append(slab)
        return tuple(nxt)

    start = jnp.full((PEER_HEADS, h2t.shape[1]), jnp.inf, F32)
    lax.fori_loop(0, n_top, extract, (start, start))

    v1 = [top_scr[0, r] for r in range(n_top)]
    v2 = [top_scr[1, r] for r in range(n_top)]
    sums = {ab: v1[ab[0]] + v2[ab[1]] for ab in _staircase(n_top)}
    cands = list(sums.values())
    best = None
    z = None
    ordered = []
    for r in range(n_top):
        cur = functools.reduce(jnp.maximum, cands)
        ordered.append(cur)
        if r == 0:
            best = cur
            z = jnp.ones_like(cur)
        elif r < PEER_TOPK:
            z = z + jnp.exp(cur - best)
        if r + 1 < n_top:
            cands = [jnp.where(c == cur, -jnp.inf, c) for c in cands]
    tau = 0.5 * (ordered[PEER_TOPK - 1] + ordered[PEER_TOPK])
    inv_z = 1.0 / z
    partners = []
    for a in range(PEER_TOPK):
        hits = [jnp.where(sums[(a, b)] >= tau, 1.0, 0.0)
                for b in range(PEER_TOPK) if (a + 1) * (b + 1) <= PEER_TOPK]
        partners.append(functools.reduce(jnp.add, hits))
    for h in range(PEER_HEADS):
        s1 = s_scr[0, h]
        s2 = s_scr[1, h]
        cnt = jnp.zeros_like(s1)
        for a in range(PEER_TOPK):
            cnt = jnp.where(s1 == v1[a][h:h + 1], partners[a][h:h + 1], cnt)
        rank2 = jnp.zeros_like(s2)
        for r in range(n_top):
            rank2 = rank2 + jnp.where(v2[r][h:h + 1] > s2, 1.0, 0.0)
        cnt_ref[h] = cnt
        aa_ref[h] = jnp.exp(s1 - v1[0][h:h + 1])
        r2_ref[h] = rank2
        bb_ref[h] = jnp.exp(s2 - v2[0][h:h + 1]) * inv_z[h:h + 1]


def _peer_select(h2t, wqt, keys):
    d, tokens = h2t.shape
    tm = TOKEN_TILE
    wide = jax.ShapeDtypeStruct((PEER_HEADS, PEER_KEYS, tokens), F32)
    blk = pl.BlockSpec((PEER_HEADS, PEER_KEYS, tm), lambda t: (0, 0, t))
    return pl.pallas_call(
        _peer_select_kernel,
        out_shape=(wide, wide, wide, wide),
        grid=(tokens // tm,),
        in_specs=[pl.BlockSpec((d, tm), lambda t: (0, t)),
                  pl.BlockSpec(wqt.shape, lambda t: (0, 0)),
                  pl.BlockSpec(keys.shape, lambda t: (0, 0, 0))],
        out_specs=(blk, blk, blk, blk),
        scratch_shapes=[pltpu.VMEM((2, PEER_HEADS, PEER_KEYS, tm), F32),
                        pltpu.VMEM((2, PEER_TOPK + 1, PEER_HEADS, tm), F32)],
        compiler_params=_params(1),
        name="peer_select",
    )(h2t, wqt, keys)


def _peer_expert_kernel(h2t_ref, u_ref, vt_ref, r2_ref, bb_ref, cnt_ref, aa_ref,
                        x1_ref, gt2_ref, gf_ref, o_ref,
                        acc_ref, a_ref, act_ref):
    e = pl.program_id(1)
    tm = h2t_ref.shape[1]
    rows_per_step = u_ref.shape[0] // PEER_KEYS
    pk = PACKED_ROWS

    @pl.when(e == 0)
    def _():
        acc_ref[...] = jnp.zeros_like(acc_ref)

    a_ref[...] = jnp.dot(u_ref[...], h2t_ref[...], preferred_element_type=F32)
    zero = jnp.zeros((pk, LANES), BF16)
    key_blocks = PEER_KEYS // GATE_ROWS
    row_groups = rows_per_step // GATE_GROUP
    tiles_per_block = GATE_ROWS // pk

    def gate_block(lanes, key0, row0):
        key_tiles = [slice(key0 + k * pk, key0 + (k + 1) * pk) for k in range(tiles_per_block)]
        gates = [[zero for _ in key_tiles] for _ in range(GATE_GROUP)]
        for h in range(PEER_HEADS):
            r2 = [r2_ref[h, kt, lanes].astype(BF16) for kt in key_tiles]
            bb = [bb_ref[h, kt, lanes].astype(BF16) for kt in key_tiles]
            for g in range(GATE_GROUP):
                ii = row0 + g
                cnt = jnp.broadcast_to(cnt_ref[h, ii:ii + 1, lanes], (pk, LANES)).astype(BF16)
                aa = jnp.broadcast_to(aa_ref[h, ii:ii + 1, lanes], (pk, LANES)).astype(BF16)
                for k in range(tiles_per_block):
                    gates[g][k] = gates[g][k] + jnp.where(r2[k] < cnt, bb[k], zero) * aa
        for g in range(GATE_GROUP):
            for k, kt in enumerate(key_tiles):
                rows = slice((row0 + g) * PEER_KEYS + kt.start, (row0 + g) * PEER_KEYS + kt.stop)
                act_ref[rows, lanes] = _gelu_tanh(a_ref[rows, lanes].astype(BF16)) * gates[g][k]

    for c in range(tm // LANES):
        for jb in range(key_blocks):
            for ig in range(row_groups):
                gate_block(slice(c * LANES, (c + 1) * LANES), jb * GATE_ROWS, ig * GATE_GROUP)
    acc_ref[...] += jnp.dot(vt_ref[...], act_ref[...], preferred_element_type=F32)

    @pl.when(e == pl.num_programs(1) - 1)
    def _():
        y = x1_ref[...] + gt2_ref[0] * acc_ref[...].T
        o_ref[...] = _rms_norm(y, gf_ref[...])


def _peer_expert(h2t, u, vt, r2, bb, cnt, aa, x1, gt2, gf, tokens_per_seq):
    d, tokens = h2t.shape
    n_exp = u.shape[0]
    tm = TOKEN_TILE
    eb = EXPERT_TILE
    rows = eb // PEER_KEYS
    tiles_per_seq = tokens_per_seq // tm
    full = pl.BlockSpec((PEER_HEADS, PEER_KEYS, tm), lambda t, e: (0, 0, t))
    part = pl.BlockSpec((PEER_HEADS, rows, tm), lambda t, e: (0, e, t))
    return pl.pallas_call(
        _peer_expert_kernel,
        out_shape=jax.ShapeDtypeStruct((tokens, d), F32),
        grid=(tokens // tm, n_exp // eb),
        in_specs=[pl.BlockSpec((d, tm), lambda t, e: (0, t)),
                  pl.BlockSpec((eb, d), lambda t, e: (e, 0)),
                  pl.BlockSpec((d, eb), lambda t, e: (0, e)),
                  full, full, part, part,
                  pl.BlockSpec((tm, d), lambda t, e: (t, 0)),
                  pl.BlockSpec((1, 1, d), lambda t, e: (t // tiles_per_seq, 0, 0)),
                  pl.BlockSpec((1, d), lambda t, e: (0, 0))],
        out_specs=pl.BlockSpec((tm, d), lambda t, e: (t, 0)),
        scratch_shapes=[pltpu.VMEM((d, tm), F32),
                        pltpu.VMEM((eb, tm), F32),
                        pltpu.VMEM((eb, tm), BF16)],
        compiler_params=_params(2),
        name="peer_expert",
    )(h2t, u, vt, r2, bb, cnt, aa, x1, gt2, gf)


def kernel(x, c, w_ada, b_ada, g_norm1, w_in, w_attn_up, w_pool_mix, b_pool_mix, pool_scale,
           w_pool_up, w_out, g_norm2, w_peer_q, peer_sub_keys, peer_u, peer_v, g_final):
    bsz, s, d = x.shape
    depth = w_ada.shape[0]
    assert depth == 1, "the final norm is fused into the last PEER step of a single layer"
    assert s % TOKEN_TILE == 0 and TOKEN_TILE % MOBA_BLOCK == 0
    assert w_in.shape[-1] == 3 * ATTN_WIDTH + POOL_WIDTH + 2 * d
    for l in range(depth):
        mod = _adaln(c, w_ada[l], b_ada[l])
        sh1, sc1, gt1, sh2, sc2, gt2 = [m.reshape(bsz, 1, d) for m in jnp.split(mod, 6, axis=-1)]
        q, k, vt, p, ga, gb, kmean = _inproj(x, g_norm1[l], sh1, sc1, w_in[l].astype(BF16))
        nb = s // MOBA_BLOCK
        kmean = jnp.pad(kmean.reshape(bsz, nb, ATTN_WIDTH), ((0, 0), (0, -nb % SUBLANES), (0, 0)))
        attn = _moba(q, k, vt, kmean)
        x1, h2t = _mix(x, attn, p, ga, gb,
                       w_attn_up[l].astype(BF16), w_pool_mix[l].astype(BF16),
                       b_pool_mix[l].reshape(len(POOL_WINDOWS), 1, POOL_GROUP_DIM),
                       pool_scale[l].reshape(1, POOL_WIDTH), w_pool_up[l].astype(BF16),
                       w_out[l].astype(BF16), gt1, g_norm2[l].reshape(1, d), sh2, sc2)
        r2, bb, cnt, aa = _peer_select(h2t, w_peer_q[l].T.astype(BF16), peer_sub_keys[l].astype(BF16))
        y = _peer_expert(h2t, peer_u[l].astype(BF16), peer_v[l].T.astype(BF16),
                         r2, bb, cnt, aa, x1.reshape(bsz * s, d), gt2, g_final.reshape(1, d), s)
        x = y.reshape(bsz, s, d)
    return x
```
